```python
import jax, jax.numpy as jnp
from jax import lax
import numpy as np

D_MODEL = 1024
BATCH = 4
SEQ = 8192
DEPTH = 1

HEAD_DIM = 64
ATTN_GROUPS = ((128, 1), (512, 4), (2048, 16))
N_GROUPS = len(ATTN_GROUPS)
ATTN_HEADS_PER_GROUP = 4
ATTN_GROUP_WIDTH = ATTN_HEADS_PER_GROUP * HEAD_DIM
Q_BLOCK = 128
ROPE_THETA = 10000.0
RWKV_HEADS = 8
RWKV_WIDTH = RWKV_HEADS * HEAD_DIM
DECAY_LORA = 32
ICLR_LORA = 32
GATE_LORA = 96
RWKV_GN_EPS = 64e-5
D_FF = 4 * D_MODEL
PLE_DIM = 256
NORM_EPS = 1e-6

ATTN_QKV_COLS = 3 * N_GROUPS * ATTN_GROUP_WIDTH
RWKV_SPLITS = (RWKV_WIDTH, 2 * RWKV_WIDTH, 3 * RWKV_WIDTH,
               3 * RWKV_WIDTH + DECAY_LORA, 3 * RWKV_WIDTH + DECAY_LORA + ICLR_LORA)
RWKV_COLS = 3 * RWKV_WIDTH + DECAY_LORA + ICLR_LORA + GATE_LORA
GATE_COLS = 2 * D_MODEL
IN_COLS = ATTN_QKV_COLS + RWKV_COLS + GATE_COLS

kernel_name = 'hybrid_dilated_attn_rwkv7_gated_block'


def rmsnorm(x, g):
    xf = x.astype(jnp.float32)
    y = xf * lax.rsqrt(jnp.mean(xf * xf, axis=-1, keepdims=True) + NORM_EPS)
    return (y * g.astype(jnp.float32)).astype(x.dtype)


def rope(x, pos):
    half = HEAD_DIM // 2
    inv_freq = ROPE_THETA ** (-jnp.arange(0, HEAD_DIM, 2, dtype=jnp.float32) / HEAD_DIM)
    ang = pos[:, None] * inv_freq[None, :]
    ang = ang.reshape((pos.shape[0],) + (1,) * (x.ndim - 3) + (half,))
    c, s = jnp.cos(ang), jnp.sin(ang)
    xf = x.astype(jnp.float32)
    x1, x2 = xf[..., :half], xf[..., half:]
    return jnp.concatenate([x1 * c - x2 * s, x2 * c + x1 * s], axis=-1).astype(x.dtype)


def token_shift(z):
    return jnp.pad(z[:, :-1], ((0, 0), (1, 0), (0, 0)))


def dilated_attention_branch(z_attn):
    B, S, _ = z_attn.shape
    qkv = z_attn.reshape(B, S, 3, N_GROUPS, ATTN_HEADS_PER_GROUP, HEAD_DIM)
    pos = jnp.arange(S, dtype=jnp.float32)
    q = rope(qkv[:, :, 0], pos)
    k = rope(qkv[:, :, 1], pos)
    v = qkv[:, :, 2]
    qs = [q[:, :, g] for g in range(N_GROUPS)]
    ks = [k[:, :, g] for g in range(N_GROUPS)]
    vs = [v[:, :, g] for g in range(N_GROUPS)]
    scale = HEAD_DIM ** -0.5

    def block(bi):
        start = bi * Q_BLOCK
        t = start + jnp.arange(Q_BLOCK)
        outs, lses = [], []
        for g, (window, dil) in enumerate(ATTN_GROUPS):
            j = jnp.arange(window // dil + 1)
            idx = t[:, None] - dil * j[None, :]
            valid = idx >= 0
            idx_c = jnp.maximum(idx, 0)
            qb = lax.dynamic_slice_in_dim(qs[g], start, Q_BLOCK, axis=1)
            kg = jnp.take(ks[g], idx_c, axis=1)
            vg = jnp.take(vs[g], idx_c, axis=1)
            s = jnp.einsum('bqhd,bqjhd->bqhj', qb, kg, preferred_element_type=jnp.float32) * scale
            s = jnp.where(valid[None, :, None, :], s, -jnp.inf)
            m = jnp.max(s, axis=-1, keepdims=True)
            e = jnp.exp(s - m)
            den = jnp.sum(e, axis=-1)
            o = jnp.einsum('bqhj,bqjhd->bqhd', e, vg.astype(jnp.float32)) / den[..., None]
            outs.append(o)
            lses.append(m[..., 0] + jnp.log(den))
        alpha = jax.nn.softmax(jnp.stack(lses, axis=0), axis=0)
        o = alpha[0][..., None] * outs[0]
        for g in range(1, N_GROUPS):
            o = o + alpha[g][..., None] * outs[g]
        return o.astype(z_attn.dtype)

    out = lax.map(block, jnp.arange(S // Q_BLOCK))
    return out.transpose(1, 0, 2, 3, 4).reshape(B, S, ATTN_GROUP_WIDTH)


def rwkv7_time_mix(z, mu, w0, w2, a0, a2, g2, k_k, k_a, r_k, ln_w, ln_b):
    B, S, _ = z.shape
    f32 = jnp.float32
    z = z + (token_shift(z) - z) * mu
    r, k, v, xw, xa, xg = jnp.split(z, RWKV_SPLITS, axis=-1)
    logw = -jax.nn.softplus(-(w0 + jnp.tanh(xw) @ w2)) - 0.5
    decay = jnp.exp(-jnp.exp(logw.astype(f32)))
    a = jax.nn.sigmoid(a0 + xa @ a2)
    g = jax.nn.sigmoid(xg) @ g2
    heads = lambda t: t.reshape(B, S, RWKV_HEADS, HEAD_DIM).astype(f32)
    kk = heads(k * k_k)
    kk = kk / jnp.maximum(jnp.sqrt(jnp.sum(kk * kk, axis=-1, keepdims=True)), 1e-12)
    k = k * (1.0 + (a - 1.0) * k_a)
    r_h, k_h, v_h, a_h, w_h = heads(r), heads(k), heads(v), heads(a), heads(decay)
    seq = tuple(jnp.moveaxis(t, 1, 0) for t in (r_h, w_h, k_h, v_h, -kk, kk * a_h))

    def step(state, inp):
        r_t, w_t, k_t, v_t, a_t, b_t = inp
        sa = jnp.einsum('bhvk,bhk->bhv', state, a_t)
        state = (state * w_t[:, :, None, :] + sa[..., None] * b_t[:, :, None, :]
                 + v_t[..., None] * k_t[:, :, None, :])
        y = jnp.einsum('bhvk,bhk->bhv', state, r_t)
        return state, y

    state0 = jnp.zeros((B, RWKV_HEADS, HEAD_DIM, HEAD_DIM), f32)
    _, y = lax.scan(step, state0, seq)
    y = jnp.moveaxis(y, 0, 1)
    mean = jnp.mean(y, axis=-1, keepdims=True)
    var = jnp.mean(jnp.square(y - mean), axis=-1, keepdims=True)
    y = (y - mean) * lax.rsqrt(var + RWKV_GN_EPS)
    y = y.reshape(B, S, RWKV_WIDTH) * ln_w.astype(f32) + ln_b.astype(f32)
    bonus = jnp.sum(r_h * k_h * r_k.astype(f32), axis=-1, keepdims=True) * v_h
    y = (y + bonus.reshape(B, S, RWKV_WIDTH)) * g.astype(f32)
    return y.astype(z.dtype)


def setup_inputs(seed: int = 0) -> dict:
    key = jax.random.key(seed)
    ks = jax.random.split(key, 32)
    f32 = jnp.float32
    nrm = lambda k, shape, scale: jax.random.normal(k, shape, f32) * scale
    gain = lambda k, n: 1.0 + 0.02 * jax.random.normal(k, (DEPTH, n), f32)
    L = DEPTH
    return {
        'x': jax.random.normal(ks[0], (BATCH, SEQ, D_MODEL), f32),
        'p': jax.random.normal(ks[1], (DEPTH, BATCH, SEQ, PLE_DIM), f32),
        'mix_pre_norm': gain(ks[2], D_MODEL),
        'w_in': nrm(ks[3], (L, D_MODEL, IN_COLS), D_MODEL ** -0.5),
        'rwkv_mu': jax.random.uniform(ks[4], (L, RWKV_COLS), f32, 0.2, 0.8),
        'rwkv_w0': jax.random.uniform(ks[5], (L, RWKV_WIDTH), f32, -4.0, 0.0),
        'rwkv_w2': nrm(ks[6], (L, DECAY_LORA, RWKV_WIDTH), 0.5 * DECAY_LORA ** -0.5),
        'rwkv_a0': nrm(ks[7], (L, RWKV_WIDTH), 0.1),
        'rwkv_a2': nrm(ks[8], (L, ICLR_LORA, RWKV_WIDTH), 0.5 * ICLR_LORA ** -0.5),
        'rwkv_g2': nrm(ks[9], (L, GATE_LORA, RWKV_WIDTH), GATE_LORA ** -0.5),
        'rwkv_k_k': 0.85 + 0.02 * jax.random.normal(ks[10], (L, RWKV_WIDTH), f32),
        'rwkv_k_a': 1.0 + 0.02 * jax.random.normal(ks[11], (L, RWKV_WIDTH), f32),
        'rwkv_r_k': nrm(ks[12], (L, RWKV_HEADS, HEAD_DIM), 0.1),
        'rwkv_ln_w': gain(ks[13], RWKV_WIDTH),
        'rwkv_ln_b': nrm(ks[14], (L, RWKV_WIDTH), 0.02),
        'w_attn_up': nrm(ks[15], (L, ATTN_GROUP_WIDTH, D_MODEL), ATTN_GROUP_WIDTH ** -0.5),
        'w_rwkv_up': nrm(ks[16], (L, RWKV_WIDTH, D_MODEL), RWKV_WIDTH ** -0.5),
        'w_out': nrm(ks[17], (L, D_MODEL, D_MODEL), D_MODEL ** -0.5),
        'mix_post_norm': gain(ks[18], D_MODEL),
        'ffn_pre_norm': gain(ks[19], D_MODEL),
        'w_ff_in': nrm(ks[20], (L, D_MODEL, D_FF), D_MODEL ** -0.5),
        'w_ff_out': nrm(ks[21], (L, D_FF, D_MODEL), D_FF ** -0.5),
        'ffn_post_norm': gain(ks[22], D_MODEL),
        'w_ple': nrm(ks[23], (L, PLE_DIM, D_MODEL), PLE_DIM ** -0.5),
        'w_ple_gate': nrm(ks[24], (L, D_MODEL, D_MODEL), D_MODEL ** -0.5),
        'ple_post_norm': gain(ks[25], D_MODEL),
    }


def reference(x, p, mix_pre_norm, w_in, rwkv_mu, rwkv_w0, rwkv_w2, rwkv_a0, rwkv_a2, rwkv_g2,
              rwkv_k_k, rwkv_k_a, rwkv_r_k, rwkv_ln_w, rwkv_ln_b, w_attn_up, w_rwkv_up, w_out,
              mix_post_norm, ffn_pre_norm, w_ff_in, w_ff_out, ffn_post_norm, w_ple, w_ple_gate,
              ple_post_norm):
    h = x
    for i in range(DEPTH):
        u = rmsnorm(h, mix_pre_norm[i])
        z = u @ w_in[i]
        z_attn = z[..., :ATTN_QKV_COLS]
        z_rwkv = z[..., ATTN_QKV_COLS:ATTN_QKV_COLS + RWKV_COLS]
        z_gate = z[..., ATTN_QKV_COLS + RWKV_COLS:]
        o_attn = dilated_attention_branch(z_attn)
        o_rwkv = rwkv7_time_mix(z_rwkv, rwkv_mu[i], rwkv_w0[i], rwkv_w2[i], rwkv_a0[i],
                                rwkv_a2[i], rwkv_g2[i], rwkv_k_k[i], rwkv_k_a[i], rwkv_r_k[i],
                                rwkv_ln_w[i], rwkv_ln_b[i])
        gate_attn, gate_rwkv = jnp.split(jax.nn.sigmoid(z_gate), 2, axis=-1)
        merged = gate_attn * (o_attn @ w_attn_up[i]) + gate_rwkv * (o_rwkv @ w_rwkv_up[i])
        h = h + rmsnorm(merged @ w_out[i], mix_post_norm[i])
        f = rmsnorm(h, ffn_pre_norm[i])
        f = jnp.square(jax.nn.relu(f @ w_ff_in[i])) @ w_ff_out[i]
        h = h + rmsnorm(f, ffn_post_norm[i])
        e = (p[i] @ w_ple[i]) * jax.nn.sigmoid(h @ w_ple_gate[i])
        h = h + rmsnorm(e, ple_post_norm[i])
    return h
```

```python
import functools

import numpy as np
import jax
import jax.numpy as jnp
from jax import lax
from jax.experimental import pallas as pl
from jax.experimental.pallas import tpu as pltpu

F32 = jnp.float32
BF16 = jnp.bfloat16

HEAD_DIM = 64
ATTN_GROUPS = ((128, 1), (512, 4), (2048, 16))
N_GROUPS = 3
ATTN_W = 256
ATTN_KEYS = 128
ROPE_THETA = 10000.0
RWKV_W = 512
RWKV_HEADS = 8
DECAY_LORA, ICLR_LORA, GATE_LORA = 32, 32, 96
LORA_SLOT = 128
RWKV_GN_EPS = 64e-5
NORM_EPS = 1e-6
CHUNK = 64
NEG = -1e30

VMEM_LIMIT = 56 * 1024 * 1024


def _cparams(sem):
    return pltpu.CompilerParams(dimension_semantics=sem, vmem_limit_bytes=VMEM_LIMIT)


def _const_spec(shape):
    nd = len(shape)
    return pl.BlockSpec(shape, lambda *_: (0,) * nd)


def _dot(a, b):
    return jnp.dot(a, b, preferred_element_type=F32)


def _dot_nt(a, b):
    return lax.dot_general(a, b, (((1,), (1,)), ((), ())), preferred_element_type=F32)


def _dot_tn(a, b):
    return lax.dot_general(a, b, (((0,), (0,)), ((), ())), preferred_element_type=F32)


def _split3(x):
    hi = x.astype(BF16)
    r1 = x - hi.astype(F32)
    mid = r1.astype(BF16)
    lo = (r1 - mid.astype(F32)).astype(BF16)
    return hi, mid, lo


def _dot_exact_rhs(x, rhs_bf16, passes):
    parts = _split3(x)[:passes]
    acc = _dot(parts[0], rhs_bf16)
    for p in parts[1:]:
        acc = acc + _dot(p, rhs_bf16)
    return acc


def _exact_lhs_dot(lhs_bf16, x, passes):
    parts = _split3(x)[:passes]
    acc = _dot(lhs_bf16, parts[0])
    for p in parts[1:]:
        acc = acc + _dot(lhs_bf16, p)
    return acc


def _rms(x, g):
    ms = jnp.mean(x * x, axis=-1, keepdims=True)
    return x * lax.rsqrt(ms + NORM_EPS) * g


def _in_proj_kernel(x_ref, g_ref, wqk_ref, wv_ref, wr_ref, wl_ref, wg_ref, rope_ref,
                    mur_ref, mul_ref, qkv_ref, rkv_ref, lora_ref, gate_ref,
                    carry_r, carry_l):
    s = pl.program_id(1)
    tm = x_ref.shape[0]
    u = _rms(x_ref[...], g_ref[...]).astype(BF16)

    zqk = _dot(u, wqk_ref[...])
    nqk = zqk.shape[1] // 128
    lane = lax.broadcasted_iota(jnp.int32, (1, 128), 1)
    first_half = (lane % HEAD_DIM) < (HEAD_DIM // 2)
    for c in range(nqk):
        t0 = 0 if c < nqk // 2 else 256
        cos_t = rope_ref[:, t0:t0 + 128]
        sin_t = rope_ref[:, t0 + 128:t0 + 256]
        xc = zqk[:, c * 128:(c + 1) * 128]
        rot = jnp.where(first_half, pltpu.roll(xc, 96, 1), pltpu.roll(xc, 32, 1))
        qkv_ref[:, c * 128:(c + 1) * 128] = (xc * cos_t + rot * sin_t).astype(BF16)
    nq = zqk.shape[1]
    qkv_ref[:, nq:] = _dot(u, wv_ref[...]).astype(BF16)

    row0 = lax.broadcasted_iota(jnp.int32, (tm, 1), 0) == 0

    @pl.when(s == 0)
    def _():
        carry_r[...] = jnp.zeros_like(carry_r)
        carry_l[...] = jnp.zeros_like(carry_l)

    def shift_lerp(z, carry, mu):
        zs = jnp.where(row0, carry[0:1, :], pltpu.roll(z, 1, 0))
        carry[0:1, :] = z[tm - 1:tm, :]
        return z + (zs - z) * mu

    rkv_ref[...] = shift_lerp(_dot(u, wr_ref[...]), carry_r, mur_ref[...])
    lora_ref[...] = shift_lerp(_dot(u, wl_ref[...]), carry_l, mul_ref[...])

    gate_ref[...] = jax.nn.sigmoid(_dot(u, wg_ref[...])).astype(BF16)


def _in_proj(x, g, wqk, wv, wr, wl, wg, rope, mur, mul, tm):
    B, S, D = x.shape
    nqkv = wqk.shape[1] + wv.shape[1]
    row = lambda w: pl.BlockSpec((None, tm, w), lambda b, s: (b, s, 0))
    return pl.pallas_call(
        _in_proj_kernel,
        grid=(B, S // tm),
        in_specs=[row(D), _const_spec(g.shape), _const_spec(wqk.shape), _const_spec(wv.shape),
                  _const_spec(wr.shape), _const_spec(wl.shape), _const_spec(wg.shape),
                  pl.BlockSpec((tm, rope.shape[1]), lambda b, s: (s, 0)),
                  _const_spec(mur.shape), _const_spec(mul.shape)],
        out_specs=[row(nqkv), row(wr.shape[1]), row(wl.shape[1]), row(wg.shape[1])],
        out_shape=[jax.ShapeDtypeStruct((B, S, nqkv), BF16),
                   jax.ShapeDtypeStruct((B, S, wr.shape[1]), F32),
                   jax.ShapeDtypeStruct((B, S, wl.shape[1]), F32),
                   jax.ShapeDtypeStruct((B, S, wg.shape[1]), BF16)],
        scratch_shapes=[pltpu.VMEM((8, wr.shape[1]), F32), pltpu.VMEM((8, wl.shape[1]), F32)],
        compiler_params=_cparams(("arbitrary", "arbitrary")),
        name="in_proj",
    )(x, g, wqk, wv, wr, wl, wg, rope, mur, mul)


def _attn_kernel(q_ref, kp_ref, kc_ref, vp_ref, vc_ref, o_ref, lse_ref):
    i = pl.program_id(2)
    tq = q_ref.shape[0]
    nsub = tq // ATTN_KEYS
    kw = 2 * ATTN_KEYS
    col = lax.broadcasted_iota(jnp.int32, (ATTN_KEYS, kw), 1)
    rowi = lax.broadcasted_iota(jnp.int32, (ATTN_KEYS, kw), 0)
    diff = col - rowi
    band = (diff >= 0) & (diff <= ATTN_KEYS)
    band_first = band & ((col >= ATTN_KEYS) | (i > 0))
    lane = lax.broadcasted_iota(jnp.int32, (1, ATTN_W), 1)
    hmask = [(lane // HEAD_DIM) == h for h in range(ATTN_W // HEAD_DIM)]

    for sb in range(nsub):
        q = q_ref[sb * ATTN_KEYS:(sb + 1) * ATTN_KEYS, :]
        if sb == 0:
            k = jnp.concatenate([kp_ref[...], kc_ref[0:ATTN_KEYS, :]], axis=0)
            v = jnp.concatenate([vp_ref[...], vc_ref[0:ATTN_KEYS, :]], axis=0)
            valid = band_first
        else:
            k = kc_ref[(sb - 1) * ATTN_KEYS:(sb + 1) * ATTN_KEYS, :]
            v = vc_ref[(sb - 1) * ATTN_KEYS:(sb + 1) * ATTN_KEYS, :]
            valid = band
        o_acc = jnp.zeros((ATTN_KEYS, ATTN_W), F32)
        l_acc = jnp.zeros((ATTN_KEYS, ATTN_W), F32)
        for h, hm in enumerate(hmask):
            qh = jnp.where(hm, q, jnp.zeros_like(q))
            sc = jnp.where(valid, _dot_nt(qh, k), NEG)
            m = jnp.max(sc, axis=-1, keepdims=True)
            e = jnp.exp(sc - m)
            den = jnp.sum(e, axis=-1, keepdims=True)
            oh = _dot(e.astype(BF16), v)
            o_acc = jnp.where(hm, oh / den, o_acc)
            l_acc = jnp.where(hm, m + jnp.log(den), l_acc)
        o_ref[sb * ATTN_KEYS:(sb + 1) * ATTN_KEYS, :] = o_acc
        lse_ref[sb * ATTN_KEYS:(sb + 1) * ATTN_KEYS, :] = l_acc


def _attn_group(qkv, g, dil):
    B, S, C = qkv.shape
    n = S // dil
    nblk_cols = C // ATTN_W
    view = qkv.reshape(B, n, dil * C)
    tq = min(512, n)
    nsub = tq // ATTN_KEYS
    qi, ki, vi = g, N_GROUPS + g, 2 * N_GROUPS + g
    cur = lambda c: pl.BlockSpec((None, tq, ATTN_W), lambda b, r, i: (b, i, r * nblk_cols + c))
    prev = lambda c: pl.BlockSpec(
        (None, ATTN_KEYS, ATTN_W),
        lambda b, r, i: (b, jnp.maximum(i * nsub - 1, 0), r * nblk_cols + c))
    out = pl.BlockSpec((None, tq, ATTN_W), lambda b, r, i: (b, i, r))
    o, lse = pl.pallas_call(
        _attn_kernel,
        grid=(B, dil, n // tq),
        in_specs=[cur(qi), prev(ki), cur(ki), prev(vi), cur(vi)],
        out_specs=[out, out],
        out_shape=[jax.ShapeDtypeStruct((B, n, dil * ATTN_W), F32)] * 2,
        compiler_params=_cparams(("arbitrary", "arbitrary", "arbitrary")),
        name=f"attn_d{dil}",
    )(view, view, view, view, view)
    return o.reshape(B, S, ATTN_W), lse.reshape(B, S, ATTN_W)


def _bd(x):
    lane = lax.broadcasted_iota(jnp.int32, (1, x.shape[1]), 1)
    m0 = (lane % 128) < HEAD_DIM
    z = jnp.zeros_like(x)
    return jnp.concatenate([jnp.where(m0, x, z), jnp.where(m0, z, x)], axis=0)


def _rwkv_kernel(rkv_ref, lora_ref, lw_ref, vec_ref, ones_ref, tri_ref,
                 o_ref, st_ref, y_scr):
    s = pl.program_id(1)
    tt = rkv_ref.shape[0]
    nchunk = tt // CHUNK
    npair = RWKV_W // 128

    @pl.when(s == 0)
    def _():
        st_ref[...] = jnp.zeros_like(st_ref)

    w0, a0, k_k, k_a, r_k, ln_w, ln_b = [vec_ref[j:j + 1, :] for j in range(7)]
    r = rkv_ref[:, 0:RWKV_W]
    k = rkv_ref[:, RWKV_W:2 * RWKV_W]
    v = rkv_ref[:, 2 * RWKV_W:3 * RWKV_W]
    xw = lora_ref[:, 0:LORA_SLOT]
    xa = lora_ref[:, LORA_SLOT:2 * LORA_SLOT]
    xg = lora_ref[:, 2 * LORA_SLOT:3 * LORA_SLOT]
    ones = ones_ref[...]

    def lora_dot(x, j):
        w = lw_ref[j * LORA_SLOT:(j + 1) * LORA_SLOT, :]
        wh = w.astype(BF16)
        wl = (w - wh.astype(F32)).astype(BF16)
        xh = x.astype(BF16)
        xl = (x - xh.astype(F32)).astype(BF16)
        return _dot(xh, wh) + _dot(xh, wl) + _dot(xl, wh)

    y_w = -(w0 + lora_dot(jnp.tanh(xw), 0))
    softplus = jnp.maximum(y_w, 0.0) + jnp.log(1.0 + jnp.exp(-jnp.abs(y_w)))
    ld = -jnp.exp(-softplus - 0.5)
    a = jax.nn.sigmoid(a0 + lora_dot(xa, 1))
    g = lora_dot(jax.nn.sigmoid(xg), 2)
    kk = k * k_k
    kk = kk / jnp.maximum(jnp.sqrt(_dot_exact_rhs(kk * kk, ones, 2)), 1e-12)
    kp = k * (1.0 + (a - 1.0) * k_a)
    avec = -kk
    bvec = kk * a

    cum = _exact_lhs_dot(tri_ref[...], ld, 3)

    lane = lax.broadcasted_iota(jnp.int32, (1, 128), 1)
    m0 = lane < HEAD_DIM
    ri = lax.broadcasted_iota(jnp.int32, (CHUNK, 128), 0)
    ci = lax.broadcasted_iota(jnp.int32, (CHUNK, 128), 1) % HEAD_DIM
    strict = ci < ri
    incl = ci <= ri
    eye = ci == ri
    blk16 = (ci // 16) == (ri // 16)
    blk32 = (ci // 32) == (ri // 32)

    for c in range(nchunk):
        rows = slice(c * CHUNK, (c + 1) * CHUNK)
        for p in range(npair):
            cols = slice(p * 128, (p + 1) * 128)
            cm = cum[rows, cols]
            c_last = cm[CHUNK - 1:CHUNK, :]
            w_in = jnp.exp(cm)
            w_ex = jnp.exp(cm - ld[rows, cols])
            w_inv = jnp.exp(-cm)
            w_out = jnp.exp(c_last - cm)
            r_t = (r[rows, cols] * w_in).astype(BF16)
            a_t = (avec[rows, cols] * w_ex).astype(BF16)
            b_t = (bvec[rows, cols] * w_inv).astype(BF16)
            k_t = (kp[rows, cols] * w_inv).astype(BF16)
            b_o = (bvec[rows, cols] * w_out).astype(BF16)
            k_o = (kp[rows, cols] * w_out).astype(BF16)
            vv = v[rows, cols].astype(BF16)

            sc = _dot_nt(jnp.concatenate([a_t, r_t], axis=0),
                         jnp.concatenate([_bd(b_t), _bd(k_t)], axis=0))
            l_ab = jnp.where(strict, sc[0:CHUNK, 0:128], 0.0)
            m_ak = jnp.where(strict, sc[0:CHUNK, 128:256], 0.0)
            m_rb = jnp.where(incl, sc[CHUNK:, 0:128], 0.0)
            m_rk = jnp.where(incl, sc[CHUNK:, 128:256], 0.0)

            mv = _dot(jnp.concatenate([m_ak, m_rk], axis=0).astype(BF16), _bd(vv))
            makv, mrkv = mv[0:CHUNK], mv[CHUNK:]

            pw = jnp.where(blk16, l_ab, 0.0)
            tmat = jnp.where(eye, 1.0, 0.0)
            for it in range(4):
                pw16 = pw.astype(BF16)
                if it < 3:
                    z = _dot(pw16, _bd(jnp.concatenate([pw16, tmat.astype(BF16)], axis=1)))
                    pw = z[:, 0:128]
                    tmat = tmat + z[:, 128:256]
                else:
                    tmat = tmat + _dot(pw16, _bd(tmat.astype(BF16)))
            for off in (jnp.where(blk32 & ~blk16, l_ab, 0.0), jnp.where(blk32, 0.0, l_ab)):
                t16 = tmat.astype(BF16)
                ot = _dot(off.astype(BF16), _bd(t16))
                tmat = tmat + _dot(t16, _bd(ot.astype(BF16)))

            xh = _dot(tmat.astype(BF16), _bd(jnp.concatenate([a_t, makv.astype(BF16)], axis=1)))
            a_hat, v_hat = xh[:, 0:128], xh[:, 128:256]
            ah16, vh16 = a_hat.astype(BF16), v_hat.astype(BF16)

            mr = _dot(m_rb.astype(BF16), _bd(jnp.concatenate([ah16, vh16], axis=1)))
            r_hat = r[rows, cols] * w_in + mr[:, 0:128]
            y_intra = mr[:, 128:256] + mrkv

            gh = _dot_tn(jnp.concatenate([b_o, k_o], axis=0),
                         jnp.concatenate(
                             [jnp.concatenate([ah16, vh16], axis=1),
                              jnp.concatenate([jnp.zeros_like(vv), vv], axis=1)], axis=0))
            g_full, h_full = gh[:, 0:128], gh[:, 128:256]
            gt = jnp.where(m0, g_full[0:CHUNK], g_full[CHUNK:]) + jnp.where(eye, jnp.exp(c_last), 0.0)
            ht = jnp.where(m0, h_full[0:CHUNK], h_full[CHUNK:])

            st = st_ref[p]
            ys = _dot(jnp.concatenate([r_hat, gt], axis=0).astype(BF16), _bd(st.astype(BF16)))
            y_scr[rows, cols] = ys[0:CHUNK] + y_intra
            st_ref[p] = ys[CHUNK:] + ht

    y = y_scr[...]
    inv_n = 1.0 / HEAD_DIM
    mean = _dot_exact_rhs(y, ones, 2) * inv_n
    yc = y - mean
    var = _dot_exact_rhs(yc * yc, ones, 2) * inv_n
    yn = yc * lax.rsqrt(var + RWKV_GN_EPS) * ln_w + ln_b
    bonus = _dot_exact_rhs(r * kp * r_k, ones, 2) * v
    o_ref[...] = ((yn + bonus) * g).astype(o_ref.dtype)


def _rwkv(rkv, lora, lora_w, vecs, tt):
    B, S, _ = rkv.shape
    ones = jnp.asarray(np.kron(np.eye(RWKV_HEADS), np.ones((HEAD_DIM, HEAD_DIM))), BF16)
    tri = jnp.asarray(np.kron(np.eye(tt // CHUNK), np.tril(np.ones((CHUNK, CHUNK)))), BF16)
    row = lambda w: pl.BlockSpec((None, tt, w), lambda b, s: (b, s, 0))
    return pl.pallas_call(
        _rwkv_kernel,
        grid=(B, S // tt),
        in_specs=[row(rkv.shape[2]), row(lora.shape[2]), _const_spec(lora_w.shape),
                  _const_spec(vecs.shape), _const_spec(ones.shape), _const_spec(tri.shape)],
        out_specs=row(RWKV_W),
        out_shape=jax.ShapeDtypeStruct((B, S, RWKV_W), BF16),
        scratch_shapes=[pltpu.VMEM((RWKV_W // 128, HEAD_DIM, 128), F32),
                        pltpu.VMEM((tt, RWKV_W), F32)],
        compiler_params=_cparams(("arbitrary", "arbitrary")),
        name="rwkv7",
    )(rkv, lora, lora_w, vecs, ones, tri)


def _mix_kernel(x_ref, o0, o1, o2, l0, l1, l2, orw_ref, gate_ref,
                wau_ref, wru_ref, wout_ref, gpost_ref, h_ref):
    ls = [l0[...], l1[...], l2[...]]
    m = jnp.maximum(jnp.maximum(ls[0], ls[1]), ls[2])
    ws = [jnp.exp(l - m) for l in ls]
    o_attn = (ws[0] * o0[...] + ws[1] * o1[...] + ws[2] * o2[...]) / (ws[0] + ws[1] + ws[2])
    d = x_ref.shape[1]
    ga = gate_ref[:, 0:d].astype(F32)
    gr = gate_ref[:, d:2 * d].astype(F32)
    merged = ga * _dot(o_attn.astype(BF16), wau_ref[...]) + gr * _dot(orw_ref[...], wru_ref[...])
    mo = _dot(merged.astype(BF16), wout_ref[...])
    h_ref[...] = x_ref[...] + _rms(mo, gpost_ref[...])


def _mix(x2, os_, ls_, orw, gates, wau, wru, wout, gpost, tm):
    T, D = x2.shape
    row = lambda w: pl.BlockSpec((tm, w), lambda i: (i, 0))
    return pl.pallas_call(
        _mix_kernel,
        grid=(T // tm,),
        in_specs=[row(D)] + [row(ATTN_W)] * 6 + [row(RWKV_W), row(2 * D),
                  _const_spec(wau.shape), _const_spec(wru.shape), _const_spec(wout.shape),
                  _const_spec(gpost.shape)],
        out_specs=row(D),
        out_shape=jax.ShapeDtypeStruct((T, D), F32),
        compiler_params=_cparams(("arbitrary",)),
        name="mix",
    )(x2, *os_, *ls_, orw, gates, wau, wru, wout, gpost)


def _ffn_kernel(h_ref, p_ref, gpre_ref, wfi_ref, wfo_ref, gpost_ref, wple_ref, wpg_ref,
                gple_ref, out_ref):
    h = h_ref[...]
    f = _rms(h, gpre_ref[...]).astype(BF16)
    dff = wfi_ref.shape[1]
    nck = 4
    ck = dff // nck
    acc = jnp.zeros(h.shape, F32)
    for j in range(nck):
        a = jnp.maximum(_dot(f, wfi_ref[:, j * ck:(j + 1) * ck]), 0.0)
        acc = acc + _dot((a * a).astype(BF16), wfo_ref[j * ck:(j + 1) * ck, :])
    h = h + _rms(acc, gpost_ref[...])
    e = _dot(p_ref[...].astype(BF16), wple_ref[...]) * jax.nn.sigmoid(
        _dot(h.astype(BF16), wpg_ref[...]))
    out_ref[...] = h + _rms(e, gple_ref[...])


def _ffn(h, p2, gpre, wfi, wfo, gpost, wple, wpg, gple, tm):
    T, D = h.shape
    row = lambda w: pl.BlockSpec((tm, w), lambda i: (i, 0))
    return pl.pallas_call(
        _ffn_kernel,
        grid=(T // tm,),
        in_specs=[row(D), row(p2.shape[1]), _const_spec(gpre.shape), _const_spec(wfi.shape),
                  _const_spec(wfo.shape), _const_spec(gpost.shape), _const_spec(wple.shape),
                  _const_spec(wpg.shape), _const_spec(gple.shape)],
        out_specs=row(D),
        out_shape=jax.ShapeDtypeStruct((T, D), F32),
        compiler_params=_cparams(("arbitrary",)),
        name="ffn_ple",
    )(h, p2, gpre, wfi, wfo, gpost, wple, wpg, gple)


def _rope_tables(S):
    half = HEAD_DIM // 2
    inv_freq = ROPE_THETA ** (-jnp.arange(0, HEAD_DIM, 2, dtype=F32) / HEAD_DIM)
    ang = jnp.arange(S, dtype=F32)[:, None] * inv_freq[None, :]
    c, s = jnp.cos(ang), jnp.sin(ang)
    cos_t = jnp.tile(jnp.concatenate([c, c], axis=1), (1, 2))
    sin_t = jnp.tile(jnp.concatenate([-s, s], axis=1), (1, 2))
    scale = HEAD_DIM ** -0.5
    return jnp.concatenate([cos_t * scale, sin_t * scale, cos_t, sin_t], axis=1)


def _pad_slots(parts, axis):
    out = []
    for a in parts:
        pad = [(0, 0)] * a.ndim
        pad[axis] = (0, LORA_SLOT - a.shape[axis])
        out.append(jnp.pad(a, pad))
    return jnp.concatenate(out, axis=axis)


def kernel(x, p, mix_pre_norm, w_in, rwkv_mu, rwkv_w0, rwkv_w2, rwkv_a0, rwkv_a2, rwkv_g2,
           rwkv_k_k, rwkv_k_a, rwkv_r_k, rwkv_ln_w, rwkv_ln_b, w_attn_up, w_rwkv_up, w_out,
           mix_post_norm, ffn_pre_norm, w_ff_in, w_ff_out, ffn_post_norm, w_ple, w_ple_gate,
           ple_post_norm):
    B, S, D = x.shape
    depth = w_in.shape[0]
    nqk = 2 * N_GROUPS * ATTN_W
    nqkv = 3 * N_GROUPS * ATTN_W
    nr = 3 * RWKV_W
    lo = nqkv + nr
    h = x
    rope = _rope_tables(S)
    for i in range(depth):
        w = w_in[i]
        wl = _pad_slots([w[:, lo:lo + DECAY_LORA],
                         w[:, lo + DECAY_LORA:lo + DECAY_LORA + ICLR_LORA],
                         w[:, lo + DECAY_LORA + ICLR_LORA:lo + DECAY_LORA + ICLR_LORA + GATE_LORA]], 1)
        ng = lo + DECAY_LORA + ICLR_LORA + GATE_LORA
        mu = rwkv_mu[i]
        mul = _pad_slots([mu[nr:nr + DECAY_LORA], mu[nr + DECAY_LORA:nr + DECAY_LORA + ICLR_LORA],
                          mu[nr + DECAY_LORA + ICLR_LORA:]], 0)
        qkv, rkv, lora, gates = _in_proj(
            h, mix_pre_norm[i][None, :], w[:, :nqk].astype(BF16), w[:, nqk:nqkv].astype(BF16),
            w[:, nqkv:lo].astype(BF16), wl.astype(BF16), w[:, ng:].astype(BF16), rope,
            mu[None, :nr], mul[None, :], tm=min(256, S))

        os_, ls_ = [], []
        for g, (_, dil) in enumerate(ATTN_GROUPS):
            o, lse = _attn_group(qkv, g, dil)
            os_.append(o.reshape(B * S, ATTN_W))
            ls_.append(lse.reshape(B * S, ATTN_W))

        lora_w = _pad_slots([rwkv_w2[i], rwkv_a2[i], rwkv_g2[i]], 0)
        vecs = jnp.stack([rwkv_w0[i], rwkv_a0[i], rwkv_k_k[i], rwkv_k_a[i],
                          rwkv_r_k[i].reshape(-1), rwkv_ln_w[i], rwkv_ln_b[i],
                          jnp.zeros((RWKV_W,), F32)])
        orw = _rwkv(rkv, lora, lora_w, vecs, tt=min(256, S))

        tm = min(512, B * S)
        h1 = _mix(h.reshape(B * S, D), os_, ls_, orw.reshape(B * S, RWKV_W),
                  gates.reshape(B * S, 2 * D), w_attn_up[i].astype(BF16),
                  w_rwkv_up[i].astype(BF16), w_out[i].astype(BF16), mix_post_norm[i][None, :], tm)
        h2 = _ffn(h1, p[i].reshape(B * S, -1), ffn_pre_norm[i][None, :], w_ff_in[i].astype(BF16),
                  w_ff_out[i].astype(BF16), ffn_post_norm[i][None, :], w_ple[i].astype(BF16),
                  w_ple_gate[i].astype(BF16), ple_post_norm[i][None, :], tm)
        h = h2.reshape(B, S, D)
    return h
```

```python
import functools

import numpy as np
import jax
import jax.numpy as jnp
from jax import lax
from jax.experimental import pallas as pl
from jax.experimental.pallas import tpu as pltpu

F32 = jnp.float32
BF16 = jnp.bfloat16

HEAD_DIM = 64
ATTN_GROUPS = ((128, 1), (512, 4), (2048, 16))
N_GROUPS = 3
ATTN_W = 256
ATTN_KEYS = 128
ROPE_THETA = 10000.0
RWKV_W = 512
RWKV_HEADS = 8
DECAY_LORA, ICLR_LORA, GATE_LORA = 32, 32, 96
LORA_SLOT = 128
RWKV_GN_EPS = 64e-5
NORM_EPS = 1e-6
CHUNK = 64
NEG = -1e30

VMEM_LIMIT = 56 * 1024 * 1024


def _cparams(sem):
    return pltpu.CompilerParams(dimension_semantics=sem, vmem_limit_bytes=VMEM_LIMIT)


def _const_spec(shape):
    nd = len(shape)
    return pl.BlockSpec(shape, lambda *_: (0,) * nd)


def _dot(a, b):
    return jnp.dot(a, b, preferred_element_type=F32)


def _dot_nt(a, b):
    return lax.dot_general(a, b, (((1,), (1,)), ((), ())), preferred_element_type=F32)


def _dot_tn(a, b):
    return lax.dot_general(a, b, (((0,), (0,)), ((), ())), preferred_element_type=F32)


def _split3(x):
    hi = x.astype(BF16)
    r1 = x - hi.astype(F32)
    mid = r1.astype(BF16)
    lo = (r1 - mid.astype(F32)).astype(BF16)
    return hi, mid, lo


def _dot_exact_rhs(x, rhs_bf16, passes):
    parts = _split3(x)[:passes]
    acc = _dot(parts[0], rhs_bf16)
    for p in parts[1:]:
        acc = acc + _dot(p, rhs_bf16)
    return acc


def _exact_lhs_dot(lhs_bf16, x, passes):
    parts = _split3(x)[:passes]
    acc = _dot(lhs_bf16, parts[0])
    for p in parts[1:]:
        acc = acc + _dot(lhs_bf16, p)
    return acc


def _rms(x, g):
    ms = jnp.mean(x * x, axis=-1, keepdims=True)
    return x * lax.rsqrt(ms + NORM_EPS) * g


def _in_proj_kernel(x_ref, g_ref, wqk_ref, wv_ref, wr_ref, wl_ref, wg_ref, rope_ref,
                    mur_ref, mul_ref, qkv_ref, rkv_ref, lora_ref, gate_ref,
                    carry_r, carry_l):
    s = pl.program_id(1)
    tm = x_ref.shape[0]
    u = _rms(x_ref[...], g_ref[...]).astype(BF16)

    zqk = _dot(u, wqk_ref[...])
    nqk = zqk.shape[1] // 128
    lane = lax.broadcasted_iota(jnp.int32, (1, 128), 1)
    first_half = (lane % HEAD_DIM) < (HEAD_DIM // 2)
    for c in range(nqk):
        t0 = 0 if c < nqk // 2 else 256
        cos_t = rope_ref[:, t0:t0 + 128]
        sin_t = rope_ref[:, t0 + 128:t0 + 256]
        xc = zqk[:, c * 128:(c + 1) * 128]
        rot = jnp.where(first_half, pltpu.roll(xc, 96, 1), pltpu.roll(xc, 32, 1))
        qkv_ref[:, c * 128:(c + 1) * 128] = (xc * cos_t + rot * sin_t).astype(BF16)
    nq = zqk.shape[1]
    qkv_ref[:, nq:] = _dot(u, wv_ref[...]).astype(BF16)

    row0 = lax.broadcasted_iota(jnp.int32, (tm, 1), 0) == 0

    @pl.when(s == 0)
    def _():
        carry_r[...] = jnp.zeros_like(carry_r)
        carry_l[...] = jnp.zeros_like(carry_l)

    def shift_lerp(z, carry, mu):
        zs = jnp.where(row0, carry[0:1, :], pltpu.roll(z, 1, 0))
        carry[0:1, :] = z[tm - 1:tm, :]
        return z + (zs - z) * mu

    rkv_ref[...] = shift_lerp(_dot(u, wr_ref[...]), carry_r, mur_ref[...])
    lora_ref[...] = shift_lerp(_dot(u, wl_ref[...]), carry_l, mul_ref[...])

    gate_ref[...] = jax.nn.sigmoid(_dot(u, wg_ref[...])).astype(BF16)


def _in_proj(x, g, wqk, wv, wr, wl, wg, rope, mur, mul, tm):
    B, S, D = x.shape
    nqkv = wqk.shape[1] + wv.shape[1]
    row = lambda w: pl.BlockSpec((None, tm, w), lambda b, s: (b, s, 0))
    return pl.pallas_call(
        _in_proj_kernel,
        grid=(B, S // tm),
        in_specs=[row(D), _const_spec(g.shape), _const_spec(wqk.shape), _const_spec(wv.shape),
                  _const_spec(wr.shape), _const_spec(wl.shape), _const_spec(wg.shape),
                  pl.BlockSpec((tm, rope.shape[1]), lambda b, s: (s, 0)),
                  _const_spec(mur.shape), _const_spec(mul.shape)],
        out_specs=[row(nqkv), row(wr.shape[1]), row(wl.shape[1]), row(wg.shape[1])],
        out_shape=[jax.ShapeDtypeStruct((B, S, nqkv), BF16),
                   jax.ShapeDtypeStruct((B, S, wr.shape[1]), F32),
                   jax.ShapeDtypeStruct((B, S, wl.shape[1]), F32),
                   jax.ShapeDtypeStruct((B, S, wg.shape[1]), BF16)],
        scratch_shapes=[pltpu.VMEM((8, wr.shape[1]), F32), pltpu.VMEM((8, wl.shape[1]), F32)],
        compiler_params=_cparams(("arbitrary", "arbitrary")),
        name="in_proj",
    )(x, g, wqk, wv, wr, wl, wg, rope, mur, mul)


def _attn_kernel(q_ref, kp_ref, kc_ref, vp_ref, vc_ref, o_ref, lse_ref):
    i = pl.program_id(2)
    tq = q_ref.shape[0]
    nsub = tq // ATTN_KEYS
    kw = 2 * ATTN_KEYS
    col = lax.broadcasted_iota(jnp.int32, (ATTN_KEYS, kw), 1)
    rowi = lax.broadcasted_iota(jnp.int32, (ATTN_KEYS, kw), 0)
    diff = col - rowi
    band = (diff >= 0) & (diff <= ATTN_KEYS)
    band_first = band & ((col >= ATTN_KEYS) | (i > 0))
    lane = lax.broadcasted_iota(jnp.int32, (1, ATTN_W), 1)
    hmask = [(lane // HEAD_DIM) == h for h in range(ATTN_W // HEAD_DIM)]

    for sb in range(nsub):
        q = q_ref[sb * ATTN_KEYS:(sb + 1) * ATTN_KEYS, :]
        if sb == 0:
            k = jnp.concatenate([kp_ref[...], kc_ref[0:ATTN_KEYS, :]], axis=0)
            v = jnp.concatenate([vp_ref[...], vc_ref[0:ATTN_KEYS, :]], axis=0)
            valid = band_first
        else:
            k = kc_ref[(sb - 1) * ATTN_KEYS:(sb + 1) * ATTN_KEYS, :]
            v = vc_ref[(sb - 1) * ATTN_KEYS:(sb + 1) * ATTN_KEYS, :]
            valid = band
        scs = [jnp.where(valid, _dot_nt(jnp.where(hm, q, jnp.zeros_like(q)), k), NEG)
               for hm in hmask]
        ms = [jnp.max(sc, axis=-1, keepdims=True) for sc in scs]
        es = [jnp.exp(sc - m) for sc, m in zip(scs, ms)]
        dens = [jnp.sum(e, axis=-1, keepdims=True) for e in es]
        ohs = [_dot(e.astype(BF16), v) for e in es]
        o_acc = jnp.zeros((ATTN_KEYS, ATTN_W), F32)
        l_acc = jnp.zeros((ATTN_KEYS, ATTN_W), F32)
        for hm, oh, den, m in zip(hmask, ohs, dens, ms):
            o_acc = jnp.where(hm, oh / den, o_acc)
            l_acc = jnp.where(hm, m + jnp.log(den), l_acc)
        o_ref[sb * ATTN_KEYS:(sb + 1) * ATTN_KEYS, :] = o_acc
        lse_ref[sb * ATTN_KEYS:(sb + 1) * ATTN_KEYS, :] = l_acc


def _attn_group(qkv, g, dil):
    B, S, C = qkv.shape
    n = S // dil
    nblk_cols = C // ATTN_W
    view = qkv.reshape(B, n, dil * C)
    tq = min(512, n)
    nsub = tq // ATTN_KEYS
    qi, ki, vi = g, N_GROUPS + g, 2 * N_GROUPS + g
    cur = lambda c: pl.BlockSpec((None, tq, ATTN_W), lambda b, r, i: (b, i, r * nblk_cols + c))
    prev = lambda c: pl.BlockSpec(
        (None, ATTN_KEYS, ATTN_W),
        lambda b, r, i: (b, jnp.maximum(i * nsub - 1, 0), r * nblk_cols + c))
    out = pl.BlockSpec((None, tq, ATTN_W), lambda b, r, i: (b, i, r))
    o, lse = pl.pallas_call(
        _attn_kernel,
        grid=(B, dil, n // tq),
        in_specs=[cur(qi), prev(ki), cur(ki), prev(vi), cur(vi)],
        out_specs=[out, out],
        out_shape=[jax.ShapeDtypeStruct((B, n, dil * ATTN_W), F32)] * 2,
        compiler_params=_cparams(("arbitrary", "arbitrary", "arbitrary")),
        name=f"attn_d{dil}",
    )(view, view, view, view, view)
    return o.reshape(B, S, ATTN_W), lse.reshape(B, S, ATTN_W)


def _bd(x):
    lane = lax.broadcasted_iota(jnp.int32, (1, x.shape[1]), 1)
    m0 = (lane % 128) < HEAD_DIM
    z = jnp.zeros_like(x)
    return jnp.concatenate([jnp.where(m0, x, z), jnp.where(m0, z, x)], axis=0)


def _rwkv_kernel(rkv_ref, lora_ref, lw_ref, vec_ref, ones_ref, tri_ref,
                 o_ref, st_ref, y_scr):
    s = pl.program_id(1)
    tt = rkv_ref.shape[0]
    nchunk = tt // CHUNK
    npair = RWKV_W // 128

    @pl.when(s == 0)
    def _():
        st_ref[...] = jnp.zeros_like(st_ref)

    w0, a0, k_k, k_a, r_k, ln_w, ln_b = [vec_ref[j:j + 1, :] for j in range(7)]
    r = rkv_ref[:, 0:RWKV_W]
    k = rkv_ref[:, RWKV_W:2 * RWKV_W]
    v = rkv_ref[:, 2 * RWKV_W:3 * RWKV_W]
    xw = lora_ref[:, 0:LORA_SLOT]
    xa = lora_ref[:, LORA_SLOT:2 * LORA_SLOT]
    xg = lora_ref[:, 2 * LORA_SLOT:3 * LORA_SLOT]
    ones = ones_ref[...]

    def lora_dot(x, j):
        w = lw_ref[j * LORA_SLOT:(j + 1) * LORA_SLOT, :]
        wh = w.astype(BF16)
        wl = (w - wh.astype(F32)).astype(BF16)
        xh = x.astype(BF16)
        xl = (x - xh.astype(F32)).astype(BF16)
        return _dot(xh, wh) + _dot(xh, wl) + _dot(xl, wh)

    y_w = -(w0 + lora_dot(jnp.tanh(xw), 0))
    softplus = jnp.maximum(y_w, 0.0) + jnp.log(1.0 + jnp.exp(-jnp.abs(y_w)))
    ld = -jnp.exp(-softplus - 0.5)
    a = jax.nn.sigmoid(a0 + lora_dot(xa, 1))
    g = lora_dot(jax.nn.sigmoid(xg), 2)
    kk = k * k_k
    kk = kk / jnp.maximum(jnp.sqrt(_dot_exact_rhs(kk * kk, ones, 2)), 1e-12)
    kp = k * (1.0 + (a - 1.0) * k_a)
    avec = -kk
    bvec = kk * a

    cum = _exact_lhs_dot(tri_ref[...], ld, 3)

    lane = lax.broadcasted_iota(jnp.int32, (1, 128), 1)
    m0 = lane < HEAD_DIM
    ri = lax.broadcasted_iota(jnp.int32, (CHUNK, 128), 0)
    ci = lax.broadcasted_iota(jnp.int32, (CHUNK, 128), 1) % HEAD_DIM
    strict = ci < ri
    incl = ci <= ri
    eye = ci == ri
    blk16 = (ci // 16) == (ri // 16)
    blk32 = (ci // 32) == (ri // 32)

    units = [(slice(c * CHUNK, (c + 1) * CHUNK), slice(p * 128, (p + 1) * 128))
             for c in range(nchunk) for p in range(npair)]
    each = lambda fn, *lists: [fn(*args) for args in zip(*lists)]
    cat0 = lambda *xs: jnp.concatenate(xs, axis=0)
    cat1 = lambda *xs: jnp.concatenate(xs, axis=1)

    def prep(u):
        rows, cols = u
        cm = cum[rows, cols]
        c_last = cm[CHUNK - 1:CHUNK, :]
        w_in = jnp.exp(cm)
        w_ex = jnp.exp(cm - ld[rows, cols])
        w_inv = jnp.exp(-cm)
        w_out = jnp.exp(c_last - cm)
        r_f = r[rows, cols] * w_in
        return dict(
            r_f=r_f, r_t=r_f.astype(BF16),
            a_t=(avec[rows, cols] * w_ex).astype(BF16),
            b_t=(bvec[rows, cols] * w_inv).astype(BF16),
            k_t=(kp[rows, cols] * w_inv).astype(BF16),
            b_o=(bvec[rows, cols] * w_out).astype(BF16),
            k_o=(kp[rows, cols] * w_out).astype(BF16),
            vv=v[rows, cols].astype(BF16),
            wc=jnp.where(eye, jnp.exp(c_last), 0.0))

    P = each(prep, units)
    get = lambda key: [d[key] for d in P]

    sc = each(lambda d: _dot_nt(cat0(d["a_t"], d["r_t"]), cat0(_bd(d["b_t"]), _bd(d["k_t"]))), P)
    l_ab = [jnp.where(strict, x[0:CHUNK, 0:128], 0.0) for x in sc]
    m_ak = [jnp.where(strict, x[0:CHUNK, 128:256], 0.0) for x in sc]
    m_rb = [jnp.where(incl, x[CHUNK:, 0:128], 0.0).astype(BF16) for x in sc]
    m_rk = [jnp.where(incl, x[CHUNK:, 128:256], 0.0) for x in sc]

    mv = each(lambda ak, rk, d: _dot(cat0(ak, rk).astype(BF16), _bd(d["vv"])), m_ak, m_rk, P)
    makv = [x[0:CHUNK].astype(BF16) for x in mv]
    mrkv = [x[CHUNK:] for x in mv]

    pw = [jnp.where(blk16, x, 0.0).astype(BF16) for x in l_ab]
    tmat = [jnp.where(eye, 1.0, 0.0) for _ in units]
    for it in range(4):
        if it < 3:
            z = each(lambda a, t: _dot(a, _bd(cat1(a, t.astype(BF16)))), pw, tmat)
            pw = [x[:, 0:128].astype(BF16) for x in z]
            tmat = each(lambda t, x: t + x[:, 128:256], tmat, z)
        else:
            tmat = each(lambda a, t: t + _dot(a, _bd(t.astype(BF16))), pw, tmat)
    for sel in (lambda x: jnp.where(blk32 & ~blk16, x, 0.0), lambda x: jnp.where(blk32, 0.0, x)):
        off = [sel(x).astype(BF16) for x in l_ab]
        t16 = [t.astype(BF16) for t in tmat]
        ot = each(lambda o, t: _dot(o, _bd(t)).astype(BF16), off, t16)
        tmat = each(lambda t, t6, o: t + _dot(t6, _bd(o)), tmat, t16, ot)

    xh = each(lambda t, d, m: _dot(t.astype(BF16), _bd(cat1(d["a_t"], m))), tmat, P, makv)
    ahv = [x.astype(BF16) for x in xh]

    mr = each(lambda m, x: _dot(m, _bd(x)), m_rb, ahv)
    r_hat = each(lambda d, x: d["r_f"] + x[:, 0:128], P, mr)
    y_intra = each(lambda x, y: x[:, 128:256] + y, mr, mrkv)

    gh = each(lambda d, x: _dot_tn(cat0(d["b_o"], d["k_o"]),
                                   cat0(x, cat1(jnp.zeros_like(d["vv"]), d["vv"]))), P, ahv)
    gt = each(lambda x, d: jnp.where(m0, x[0:CHUNK, 0:128], x[CHUNK:, 0:128]) + d["wc"], gh, P)
    ht = [jnp.where(m0, x[0:CHUNK, 128:256], x[CHUNK:, 128:256]) for x in gh]
    lhs = each(lambda a, b: cat0(a, b).astype(BF16), r_hat, gt)

    st = [st_ref[p] for p in range(npair)]
    for c in range(nchunk):
        sl = slice(c * npair, (c + 1) * npair)
        ys = each(lambda a, s_: _dot(a, _bd(s_.astype(BF16))), lhs[sl], st)
        for (rows, cols), y_, yi in zip(units[sl], ys, y_intra[sl]):
            y_scr[rows, cols] = y_[0:CHUNK] + yi
        st = each(lambda y_, h_: y_[CHUNK:] + h_, ys, ht[sl])
    for p in range(npair):
        st_ref[p] = st[p]


    y = y_scr[...]
    inv_n = 1.0 / HEAD_DIM
    mean = _dot_exact_rhs(y, ones, 2) * inv_n
    yc = y - mean
    var = _dot_exact_rhs(yc * yc, ones, 2) * inv_n
    yn = yc * lax.rsqrt(var + RWKV_GN_EPS) * ln_w + ln_b
    bonus = _dot_exact_rhs(r * kp * r_k, ones, 2) * v
    o_ref[...] = ((yn + bonus) * g).astype(o_ref.dtype)


def _rwkv(rkv, lora, lora_w, vecs, tt):
    B, S, _ = rkv.shape
    ones = jnp.asarray(np.kron(np.eye(RWKV_HEADS), np.ones((HEAD_DIM, HEAD_DIM))), BF16)
    tri = jnp.asarray(np.kron(np.eye(tt // CHUNK), np.tril(np.ones((CHUNK, CHUNK)))), BF16)
    row = lambda w: pl.BlockSpec((None, tt, w), lambda b, s: (b, s, 0))
    return pl.pallas_call(
        _rwkv_kernel,
        grid=(B, S // tt),
        in_specs=[row(rkv.shape[2]), row(lora.shape[2]), _const_spec(lora_w.shape),
                  _const_spec(vecs.shape), _const_spec(ones.shape), _const_spec(tri.shape)],
        out_specs=row(RWKV_W),
        out_shape=jax.ShapeDtypeStruct((B, S, RWKV_W), BF16),
        scratch_shapes=[pltpu.VMEM((RWKV_W // 128, HEAD_DIM, 128), F32),
                        pltpu.VMEM((tt, RWKV_W), F32)],
        compiler_params=_cparams(("arbitrary", "arbitrary")),
        name="rwkv7",
    )(rkv, lora, lora_w, vecs, ones, tri)


def _mix_kernel(x_ref, o0, o1, o2, l0, l1, l2, orw_ref, gate_ref,
                wau_ref, wru_ref, wout_ref, gpost_ref, h_ref):
    ls = [l0[...], l1[...], l2[...]]
    m = jnp.maximum(jnp.maximum(ls[0], ls[1]), ls[2])
    ws = [jnp.exp(l - m) for l in ls]
    o_attn = (ws[0] * o0[...] + ws[1] * o1[...] + ws[2] * o2[...]) / (ws[0] + ws[1] + ws[2])
    d = x_ref.shape[1]
    ga = gate_ref[:, 0:d].astype(F32)
    gr = gate_ref[:, d:2 * d].astype(F32)
    merged = ga * _dot(o_attn.astype(BF16), wau_ref[...]) + gr * _dot(orw_ref[...], wru_ref[...])
    mo = _dot(merged.astype(BF16), wout_ref[...])
    h_ref[...] = x_ref[...] + _rms(mo, gpost_ref[...])


def _mix(x2, os_, ls_, orw, gates, wau, wru, wout, gpost, tm):
    T, D = x2.shape
    row = lambda w: pl.BlockSpec((tm, w), lambda i: (i, 0))
    return pl.pallas_call(
        _mix_kernel,
        grid=(T // tm,),
        in_specs=[row(D)] + [row(ATTN_W)] * 6 + [row(RWKV_W), row(2 * D),
                  _const_spec(wau.shape), _const_spec(wru.shape), _const_spec(wout.shape),
                  _const_spec(gpost.shape)],
        out_specs=row(D),
        out_shape=jax.ShapeDtypeStruct((T, D), F32),
        compiler_params=_cparams(("arbitrary",)),
        name="mix",
    )(x2, *os_, *ls_, orw, gates, wau, wru, wout, gpost)


def _ffn_kernel(h_ref, p_ref, gpre_ref, wfi_ref, wfo_ref, gpost_ref, wple_ref, wpg_ref,
                gple_ref, out_ref):
    h = h_ref[...]
    f = _rms(h, gpre_ref[...]).astype(BF16)
    dff = wfi_ref.shape[1]
    nck = 4
    ck = dff // nck
    acc = jnp.zeros(h.shape, F32)
    for j in range(nck):
        a = jnp.maximum(_dot(f, wfi_ref[:, j * ck:(j + 1) * ck]), 0.0)
        acc = acc + _dot((a * a).astype(BF16), wfo_ref[j * ck:(j + 1) * ck, :])
    h = h + _rms(acc, gpost_ref[...])
    e = _dot(p_ref[...].astype(BF16), wple_ref[...]) * jax.nn.sigmoid(
        _dot(h.astype(BF16), wpg_ref[...]))
    out_ref[...] = h + _rms(e, gple_ref[...])


def _ffn(h, p2, gpre, wfi, wfo, gpost, wple, wpg, gple, tm):
    T, D = h.shape
    row = lambda w: pl.BlockSpec((tm, w), lambda i: (i, 0))
    return pl.pallas_call(
        _ffn_kernel,
        grid=(T // tm,),
        in_specs=[row(D), row(p2.shape[1]), _const_spec(gpre.shape), _const_spec(wfi.shape),
                  _const_spec(wfo.shape), _const_spec(gpost.shape), _const_spec(wple.shape),
                  _const_spec(wpg.shape), _const_spec(gple.shape)],
        out_specs=row(D),
        out_shape=jax.ShapeDtypeStruct((T, D), F32),
        compiler_params=_cparams(("arbitrary",)),
        name="ffn_ple",
    )(h, p2, gpre, wfi, wfo, gpost, wple, wpg, gple)


def _rope_tables(S):
    half = HEAD_DIM // 2
    inv_freq = ROPE_THETA ** (-jnp.arange(0, HEAD_DIM, 2, dtype=F32) / HEAD_DIM)
    ang = jnp.arange(S, dtype=F32)[:, None] * inv_freq[None, :]
    c, s = jnp.cos(ang), jnp.sin(ang)
    cos_t = jnp.tile(jnp.concatenate([c, c], axis=1), (1, 2))
    sin_t = jnp.tile(jnp.concatenate([-s, s], axis=1), (1, 2))
    scale = HEAD_DIM ** -0.5
    return jnp.concatenate([cos_t * scale, sin_t * scale, cos_t, sin_t], axis=1)


def _pad_slots(parts, axis):
    out = []
    for a in parts:
        pad = [(0, 0)] * a.ndim
        pad[axis] = (0, LORA_SLOT - a.shape[axis])
        out.append(jnp.pad(a, pad))
    return jnp.concatenate(out, axis=axis)


def kernel(x, p, mix_pre_norm, w_in, rwkv_mu, rwkv_w0, rwkv_w2, rwkv_a0, rwkv_a2, rwkv_g2,
           rwkv_k_k, rwkv_k_a, rwkv_r_k, rwkv_ln_w, rwkv_ln_b, w_attn_up, w_rwkv_up, w_out,
           mix_post_norm, ffn_pre_norm, w_ff_in, w_ff_out, ffn_post_norm, w_ple, w_ple_gate,
           ple_post_norm):
    B, S, D = x.shape
    depth = w_in.shape[0]
    nqk = 2 * N_GROUPS * ATTN_W
    nqkv = 3 * N_GROUPS * ATTN_W
    nr = 3 * RWKV_W
    lo = nqkv + nr
    h = x
    rope = _rope_tables(S)
    for i in range(depth):
        w = w_in[i]
        wl = _pad_slots([w[:, lo:lo + DECAY_LORA],
                         w[:, lo + DECAY_LORA:lo + DECAY_LORA + ICLR_LORA],
                         w[:, lo + DECAY_LORA + ICLR_LORA:lo + DECAY_LORA + ICLR_LORA + GATE_LORA]], 1)
        ng = lo + DECAY_LORA + ICLR_LORA + GATE_LORA
        mu = rwkv_mu[i]
        mul = _pad_slots([mu[nr:nr + DECAY_LORA], mu[nr + DECAY_LORA:nr + DECAY_LORA + ICLR_LORA],
                          mu[nr + DECAY_LORA + ICLR_LORA:]], 0)
        qkv, rkv, lora, gates = _in_proj(
            h, mix_pre_norm[i][None, :], w[:, :nqk].astype(BF16), w[:, nqk:nqkv].astype(BF16),
            w[:, nqkv:lo].astype(BF16), wl.astype(BF16), w[:, ng:].astype(BF16), rope,
            mu[None, :nr], mul[None, :], tm=min(256, S))

        os_, ls_ = [], []
        for g, (_, dil) in enumerate(ATTN_GROUPS):
            o, lse = _attn_group(qkv, g, dil)
            os_.append(o.reshape(B * S, ATTN_W))
            ls_.append(lse.reshape(B * S, ATTN_W))

        lora_w = _pad_slots([rwkv_w2[i], rwkv_a2[i], rwkv_g2[i]], 0)
        vecs = jnp.stack([rwkv_w0[i], rwkv_a0[i], rwkv_k_k[i], rwkv_k_a[i],
                          rwkv_r_k[i].reshape(-1), rwkv_ln_w[i], rwkv_ln_b[i],
                          jnp.zeros((RWKV_W,), F32)])
        orw = _rwkv(rkv, lora, lora_w, vecs, tt=min(256, S))

        tm = min(512, B * S)
        h1 = _mix(h.reshape(B * S, D), os_, ls_, orw.reshape(B * S, RWKV_W),
                  gates.reshape(B * S, 2 * D), w_attn_up[i].astype(BF16),
                  w_rwkv_up[i].astype(BF16), w_out[i].astype(BF16), mix_post_norm[i][None, :], tm)
        h2 = _ffn(h1, p[i].reshape(B * S, -1), ffn_pre_norm[i][None, :], w_ff_in[i].astype(BF16),
                  w_ff_out[i].astype(BF16), ffn_post_norm[i][None, :], w_ple[i].astype(BF16),
                  w_ple_gate[i].astype(BF16), ple_post_norm[i][None, :], tm)
        h = h2.reshape(B, S, D)
    return h
```

```python
import functools

import numpy as np
import jax
import jax.numpy as jnp
from jax import lax
from jax.experimental import pallas as pl
from jax.experimental.pallas import tpu as pltpu

F32 = jnp.float32
BF16 = jnp.bfloat16

HEAD_DIM = 64
ATTN_GROUPS = ((128, 1), (512, 4), (2048, 16))
N_GROUPS = 3
ATTN_W = 256
ATTN_KEYS = 128
ROPE_THETA = 10000.0
RWKV_W = 512
RWKV_HEADS = 8
DECAY_LORA, ICLR_LORA, GATE_LORA = 32, 32, 96
LORA_SLOT = 128
RWKV_GN_EPS = 64e-5
NORM_EPS = 1e-6
CHUNK = 64
NEG = -1e30

VMEM_LIMIT = 56 * 1024 * 1024


def _cparams(sem):
    return pltpu.CompilerParams(dimension_semantics=sem, vmem_limit_bytes=VMEM_LIMIT)


def _const_spec(shape):
    nd = len(shape)
    return pl.BlockSpec(shape, lambda *_: (0,) * nd)


def _dot(a, b):
    return jnp.dot(a, b, preferred_element_type=F32)


def _dot_nt(a, b):
    return lax.dot_general(a, b, (((1,), (1,)), ((), ())), preferred_element_type=F32)


def _dot_tn(a, b):
    return lax.dot_general(a, b, (((0,), (0,)), ((), ())), preferred_element_type=F32)


def _split3(x):
    hi = x.astype(BF16)
    r1 = x - hi.astype(F32)
    mid = r1.astype(BF16)
    lo = (r1 - mid.astype(F32)).astype(BF16)
    return hi, mid, lo


def _dot_exact_rhs(x, rhs_bf16, passes):
    parts = _split3(x)[:passes]
    acc = _dot(parts[0], rhs_bf16)
    for p in parts[1:]:
        acc = acc + _dot(p, rhs_bf16)
    return acc


def _exact_lhs_dot(lhs_bf16, x, passes):
    parts = _split3(x)[:passes]
    acc = _dot(lhs_bf16, parts[0])
    for p in parts[1:]:
        acc = acc + _dot(lhs_bf16, p)
    return acc


def _rms(x, g):
    ms = jnp.mean(x * x, axis=-1, keepdims=True)
    return x * lax.rsqrt(ms + NORM_EPS) * g


def _in_proj_kernel(x_ref, g_ref, wqkv_ref, wr_ref, wl_ref, wg_ref, rope_ref,
                    mur_ref, mul_ref, qkv0_ref, qkv1_ref, qkv2_ref, rkv_ref, lora_ref, gate_ref,
                    carry_r, carry_l, qkv_scr):
    s = pl.program_id(1)
    tm = x_ref.shape[0]
    u = _rms(x_ref[...], g_ref[...]).astype(BF16)

    z = _dot(u, wqkv_ref[...])
    lane = lax.broadcasted_iota(jnp.int32, (1, 128), 1)
    first_half = (lane % HEAD_DIM) < (HEAD_DIM // 2)
    gw = 3 * ATTN_W
    per_group = gw // 128
    for c in range(z.shape[1] // 128):
        g, j = divmod(c, per_group)
        xc = z[:, c * 128:(c + 1) * 128]
        if j < 2 * (ATTN_W // 128):
            t0 = 0 if j < ATTN_W // 128 else 256
            rot = jnp.where(first_half, pltpu.roll(xc, 96, 1), pltpu.roll(xc, 32, 1))
            xc = xc * rope_ref[:, t0:t0 + 128] + rot * rope_ref[:, t0 + 128:t0 + 256]
        if g == 0:
            qkv0_ref[0, :, c * 128:(c + 1) * 128] = xc.astype(BF16)
        else:
            qkv_scr[c - per_group] = xc
    for g, out_ref in ((1, qkv1_ref), (2, qkv2_ref)):
        dil = out_ref.shape[0]
        for r in range(dil):
            for cb in range(per_group):
                out_ref[r, :, cb * 128:(cb + 1) * 128] = qkv_scr[
                    (g - 1) * per_group + cb, pl.ds(r, tm // dil, stride=dil), :].astype(BF16)

    row0 = lax.broadcasted_iota(jnp.int32, (tm, 1), 0) == 0

    @pl.when(s == 0)
    def _():
        carry_r[...] = jnp.zeros_like(carry_r)
        carry_l[...] = jnp.zeros_like(carry_l)

    def shift_lerp(z, carry, mu):
        zs = jnp.where(row0, carry[0:1, :], pltpu.roll(z, 1, 0))
        carry[0:1, :] = z[tm - 1:tm, :]
        return z + (zs - z) * mu

    rkv_ref[...] = shift_lerp(_dot(u, wr_ref[...]), carry_r, mur_ref[...])
    lora_ref[...] = shift_lerp(_dot(u, wl_ref[...]), carry_l, mul_ref[...])

    gate_ref[...] = jax.nn.sigmoid(_dot(u, wg_ref[...])).astype(BF16)


def _in_proj(x, g, wqkv, wr, wl, wg, rope, mur, mul, tm):
    B, S, D = x.shape
    gw = 3 * ATTN_W
    row = lambda w: pl.BlockSpec((None, tm, w), lambda b, s: (b, s, 0))
    dils = [d for _, d in ATTN_GROUPS]
    qkv_specs = [pl.BlockSpec((None, d, tm // d, gw), lambda b, s: (b, 0, s, 0)) for d in dils]
    qkv_shapes = [jax.ShapeDtypeStruct((B, d, S // d, gw), BF16) for d in dils]
    return pl.pallas_call(
        _in_proj_kernel,
        grid=(B, S // tm),
        in_specs=[row(D), _const_spec(g.shape), _const_spec(wqkv.shape),
                  _const_spec(wr.shape), _const_spec(wl.shape), _const_spec(wg.shape),
                  pl.BlockSpec((tm, rope.shape[1]), lambda b, s: (s, 0)),
                  _const_spec(mur.shape), _const_spec(mul.shape)],
        out_specs=qkv_specs + [row(wr.shape[1]), row(wl.shape[1]), row(wg.shape[1])],
        out_shape=qkv_shapes + [jax.ShapeDtypeStruct((B, S, wr.shape[1]), F32),
                                jax.ShapeDtypeStruct((B, S, wl.shape[1]), F32),
                                jax.ShapeDtypeStruct((B, S, wg.shape[1]), BF16)],
        scratch_shapes=[pltpu.VMEM((8, wr.shape[1]), F32), pltpu.VMEM((8, wl.shape[1]), F32),
                        pltpu.VMEM(((N_GROUPS - 1) * gw // 128, tm, 128), F32)],
        compiler_params=_cparams(("arbitrary", "arbitrary")),
        name="in_proj",
    )(x, g, wqkv, wr, wl, wg, rope, mur, mul)


def _attn_kernel(cur_ref, kp_ref, vp_ref, o_ref, lse_ref, *, dil, nsub):
    i = pl.program_id(1)
    kw = 2 * ATTN_KEYS
    col = lax.broadcasted_iota(jnp.int32, (ATTN_KEYS, kw), 1)
    rowi = lax.broadcasted_iota(jnp.int32, (ATTN_KEYS, kw), 0)
    diff = col - rowi
    band = (diff >= 0) & (diff <= ATTN_KEYS)
    lane = lax.broadcasted_iota(jnp.int32, (1, ATTN_W), 1)
    hmask = [(lane // HEAD_DIM) == h for h in range(ATTN_W // HEAD_DIM)]
    zero_q = jnp.zeros((ATTN_KEYS, ATTN_W), BF16)
    qs, ks, vs = slice(0, ATTN_W), slice(ATTN_W, 2 * ATTN_W), slice(2 * ATTN_W, 3 * ATTN_W)

    def load(u):
        r, sb = (u, 0) if nsub == 1 else ((0, u) if dil == 1 else (u // nsub, u % nsub))
        rows = pl.ds(pl.multiple_of(sb * ATTN_KEYS, ATTN_KEYS), ATTN_KEYS)
        q = cur_ref[r, rows, qs]
        if nsub == 1:
            k_lo, v_lo = kp_ref[r], vp_ref[r]
        else:
            lo = pl.ds(pl.multiple_of(jnp.maximum(sb - 1, 0) * ATTN_KEYS, ATTN_KEYS), ATTN_KEYS)
            first = sb == 0
            k_lo = jnp.where(first, kp_ref[r], cur_ref[r, lo, ks])
            v_lo = jnp.where(first, vp_ref[r], cur_ref[r, lo, vs])
        k = jnp.concatenate([k_lo, cur_ref[r, rows, ks]], axis=0)
        v = jnp.concatenate([v_lo, cur_ref[r, rows, vs]], axis=0)
        valid = band & ((col >= ATTN_KEYS) | (i > 0) | (sb > 0))
        return r, sb, q, k, v, valid

    def body(j, carry):
        us = [load(2 * j), load(2 * j + 1)]
        scs = [[jnp.where(valid, _dot_nt(jnp.where(hm, q, zero_q), k), NEG) for hm in hmask]
               for (_, _, q, k, _, valid) in us]
        ms = [[jnp.max(sc, axis=-1, keepdims=True) for sc in row] for row in scs]
        es = [[jnp.exp(sc - m) for sc, m in zip(r1, r2)] for r1, r2 in zip(scs, ms)]
        dens = [[jnp.sum(e, axis=-1, keepdims=True) for e in row] for row in es]
        ohs = [[_dot(e.astype(BF16), u_[4]) for e in row] for row, u_ in zip(es, us)]
        for (r, sb, *_), oh_r, den_r, m_r in zip(us, ohs, dens, ms):
            o_acc = jnp.zeros((ATTN_KEYS, ATTN_W), F32)
            l_acc = jnp.zeros((ATTN_KEYS, ATTN_W), F32)
            for hm, oh, den, m in zip(hmask, oh_r, den_r, m_r):
                o_acc = jnp.where(hm, oh / den, o_acc)
                l_acc = jnp.where(hm, m + jnp.log(den), l_acc)
            if dil == 1:
                dst = pl.ds(pl.multiple_of(sb * ATTN_KEYS, ATTN_KEYS), ATTN_KEYS)
            else:
                dst = pl.ds(r + sb * (ATTN_KEYS * dil), ATTN_KEYS, stride=dil)
            for cb in range(ATTN_W // 128):
                o_ref[cb, dst, :] = o_acc[:, cb * 128:(cb + 1) * 128]
                lse_ref[cb, dst, :] = l_acc[:, cb * 128:(cb + 1) * 128]
        return carry

    lax.fori_loop(0, dil * nsub // 2, body, 0)


def _attn_group(qkv_g, dil, tok):
    B, _, n, gw = qkv_g.shape
    S = n * dil
    tq = tok // dil
    nsub = tq // ATTN_KEYS
    cur = pl.BlockSpec((None, dil, tq, gw), lambda b, i: (b, 0, i, 0))
    prev = lambda c: pl.BlockSpec((None, dil, ATTN_KEYS, ATTN_W),
                                  lambda b, i: (b, 0, jnp.maximum(i * nsub - 1, 0), c))
    out = pl.BlockSpec((None, ATTN_W // 128, tok, 128), lambda b, i: (b, 0, i, 0))
    return pl.pallas_call(
        functools.partial(_attn_kernel, dil=dil, nsub=nsub),
        grid=(B, S // tok),
        in_specs=[cur, prev(1), prev(2)],
        out_specs=[out, out],
        out_shape=[jax.ShapeDtypeStruct((B, ATTN_W // 128, S, 128), F32)] * 2,
        compiler_params=_cparams(("arbitrary", "arbitrary")),
        name=f"attn_d{dil}",
    )(qkv_g, qkv_g, qkv_g)


def _bd(x):
    lane = lax.broadcasted_iota(jnp.int32, (1, x.shape[1]), 1)
    m0 = (lane % 128) < HEAD_DIM
    z = jnp.zeros_like(x)
    return jnp.concatenate([jnp.where(m0, x, z), jnp.where(m0, z, x)], axis=0)


def _rwkv_kernel(rkv_ref, lora_ref, lw_ref, vec_ref, ones_ref, tri_ref,
                 o_ref, st_ref, y_scr):
    s = pl.program_id(1)
    tt = rkv_ref.shape[0]
    nchunk = tt // CHUNK
    npair = RWKV_W // 128

    @pl.when(s == 0)
    def _():
        st_ref[...] = jnp.zeros_like(st_ref)

    w0, a0, k_k, k_a, r_k, ln_w, ln_b = [vec_ref[j:j + 1, :] for j in range(7)]
    r = rkv_ref[:, 0:RWKV_W]
    k = rkv_ref[:, RWKV_W:2 * RWKV_W]
    v = rkv_ref[:, 2 * RWKV_W:3 * RWKV_W]
    xw = lora_ref[:, 0:LORA_SLOT]
    xa = lora_ref[:, LORA_SLOT:2 * LORA_SLOT]
    xg = lora_ref[:, 2 * LORA_SLOT:3 * LORA_SLOT]
    ones = ones_ref[...]

    def lora_dot(x, j):
        w = lw_ref[j * LORA_SLOT:(j + 1) * LORA_SLOT, :]
        wh = w.astype(BF16)
        wl = (w - wh.astype(F32)).astype(BF16)
        xh = x.astype(BF16)
        xl = (x - xh.astype(F32)).astype(BF16)
        return _dot(xh, wh) + _dot(xh, wl) + _dot(xl, wh)

    y_w = -(w0 + lora_dot(jnp.tanh(xw), 0))
    softplus = jnp.maximum(y_w, 0.0) + jnp.log(1.0 + jnp.exp(-jnp.abs(y_w)))
    ld = -jnp.exp(-softplus - 0.5)
    a = jax.nn.sigmoid(a0 + lora_dot(xa, 1))
    g = lora_dot(jax.nn.sigmoid(xg), 2)
    kk = k * k_k
    kk = kk / jnp.maximum(jnp.sqrt(_dot_exact_rhs(kk * kk, ones, 2)), 1e-12)
    kp = k * (1.0 + (a - 1.0) * k_a)
    avec = -kk
    bvec = kk * a

    cum = _exact_lhs_dot(tri_ref[...], ld, 3)

    lane = lax.broadcasted_iota(jnp.int32, (1, 128), 1)
    m0 = lane < HEAD_DIM
    ri = lax.broadcasted_iota(jnp.int32, (CHUNK, 128), 0)
    ci = lax.broadcasted_iota(jnp.int32, (CHUNK, 128), 1) % HEAD_DIM
    strict = ci < ri
    incl = ci <= ri
    eye = ci == ri
    blk16 = (ci // 16) == (ri // 16)
    blk32 = (ci // 32) == (ri // 32)

    units = [(slice(c * CHUNK, (c + 1) * CHUNK), slice(p * 128, (p + 1) * 128))
             for c in range(nchunk) for p in range(npair)]
    each = lambda fn, *lists: [fn(*args) for args in zip(*lists)]
    cat0 = lambda *xs: jnp.concatenate(xs, axis=0)
    cat1 = lambda *xs: jnp.concatenate(xs, axis=1)

    def prep(u):
        rows, cols = u
        cm = cum[rows, cols]
        c_last = cm[CHUNK - 1:CHUNK, :]
        w_in = jnp.exp(cm)
        w_ex = jnp.exp(cm - ld[rows, cols])
        w_inv = jnp.exp(-cm)
        w_out = jnp.exp(c_last - cm)
        r_f = r[rows, cols] * w_in
        return dict(
            r_f=r_f, r_t=r_f.astype(BF16),
            a_t=(avec[rows, cols] * w_ex).astype(BF16),
            b_t=(bvec[rows, cols] * w_inv).astype(BF16),
            k_t=(kp[rows, cols] * w_inv).astype(BF16),
            b_o=(bvec[rows, cols] * w_out).astype(BF16),
            k_o=(kp[rows, cols] * w_out).astype(BF16),
            vv=v[rows, cols].astype(BF16),
            wc=jnp.where(eye, jnp.exp(c_last), 0.0))

    P = each(prep, units)
    get = lambda key: [d[key] for d in P]

    sc = each(lambda d: _dot_nt(cat0(d["a_t"], d["r_t"]), cat0(_bd(d["b_t"]), _bd(d["k_t"]))), P)
    l_ab = [jnp.where(strict, x[0:CHUNK, 0:128], 0.0) for x in sc]
    m_ak = [jnp.where(strict, x[0:CHUNK, 128:256], 0.0) for x in sc]
    m_rb = [jnp.where(incl, x[CHUNK:, 0:128], 0.0).astype(BF16) for x in sc]
    m_rk = [jnp.where(incl, x[CHUNK:, 128:256], 0.0) for x in sc]

    mv = each(lambda ak, rk, d: _dot(cat0(ak, rk).astype(BF16), _bd(d["vv"])), m_ak, m_rk, P)
    makv = [x[0:CHUNK].astype(BF16) for x in mv]
    mrkv = [x[CHUNK:] for x in mv]

    pw = [jnp.where(blk16, x, 0.0).astype(BF16) for x in l_ab]
    tmat = [jnp.where(eye, 1.0, 0.0) for _ in units]
    for it in range(4):
        if it < 3:
            z = each(lambda a, t: _dot(a, _bd(cat1(a, t.astype(BF16)))), pw, tmat)
            pw = [x[:, 0:128].astype(BF16) for x in z]
            tmat = each(lambda t, x: t + x[:, 128:256], tmat, z)
        else:
            tmat = each(lambda a, t: t + _dot(a, _bd(t.astype(BF16))), pw, tmat)
    for sel in (lambda x: jnp.where(blk32 & ~blk16, x, 0.0), lambda x: jnp.where(blk32, 0.0, x)):
        off = [sel(x).astype(BF16) for x in l_ab]
        t16 = [t.astype(BF16) for t in tmat]
        ot = each(lambda o, t: _dot(o, _bd(t)).astype(BF16), off, t16)
        tmat = each(lambda t, t6, o: t + _dot(t6, _bd(o)), tmat, t16, ot)

    xh = each(lambda t, d, m: _dot(t.astype(BF16), _bd(cat1(d["a_t"], m))), tmat, P, makv)
    ahv = [x.astype(BF16) for x in xh]

    mr = each(lambda m, x: _dot(m, _bd(x)), m_rb, ahv)
    r_hat = each(lambda d, x: d["r_f"] + x[:, 0:128], P, mr)
    y_intra = each(lambda x, y: x[:, 128:256] + y, mr, mrkv)

    gh = each(lambda d, x: _dot_tn(cat0(d["b_o"], d["k_o"]),
                                   cat0(x, cat1(jnp.zeros_like(d["vv"]), d["vv"]))), P, ahv)
    gt = each(lambda x, d: jnp.where(m0, x[0:CHUNK, 0:128], x[CHUNK:, 0:128]) + d["wc"], gh, P)
    ht = [jnp.where(m0, x[0:CHUNK, 128:256], x[CHUNK:, 128:256]) for x in gh]
    lhs = each(lambda a, b: cat0(a, b).astype(BF16), r_hat, gt)

    st = [st_ref[p] for p in range(npair)]
    for c in range(nchunk):
        sl = slice(c * npair, (c + 1) * npair)
        ys = each(lambda a, s_: _dot(a, _bd(s_.astype(BF16))), lhs[sl], st)
        for (rows, cols), y_, yi in zip(units[sl], ys, y_intra[sl]):
            y_scr[rows, cols] = y_[0:CHUNK] + yi
        st = each(lambda y_, h_: y_[CHUNK:] + h_, ys, ht[sl])
    for p in range(npair):
        st_ref[p] = st[p]


    y = y_scr[...]
    inv_n = 1.0 / HEAD_DIM
    mean = _dot_exact_rhs(y, ones, 2) * inv_n
    yc = y - mean
    var = _dot_exact_rhs(yc * yc, ones, 2) * inv_n
    yn = yc * lax.rsqrt(var + RWKV_GN_EPS) * ln_w + ln_b
    bonus = _dot_exact_rhs(r * kp * r_k, ones, 2) * v
    o_ref[...] = ((yn + bonus) * g).astype(o_ref.dtype)


def _rwkv(rkv, lora, lora_w, vecs, tt):
    B, S, _ = rkv.shape
    ones = jnp.asarray(np.kron(np.eye(RWKV_HEADS), np.ones((HEAD_DIM, HEAD_DIM))), BF16)
    tri = jnp.asarray(np.kron(np.eye(tt // CHUNK), np.tril(np.ones((CHUNK, CHUNK)))), BF16)
    row = lambda w: pl.BlockSpec((None, tt, w), lambda b, s: (b, s, 0))
    return pl.pallas_call(
        _rwkv_kernel,
        grid=(B, S // tt),
        in_specs=[row(rkv.shape[2]), row(lora.shape[2]), _const_spec(lora_w.shape),
                  _const_spec(vecs.shape), _const_spec(ones.shape), _const_spec(tri.shape)],
        out_specs=row(RWKV_W),
        out_shape=jax.ShapeDtypeStruct((B, S, RWKV_W), BF16),
        scratch_shapes=[pltpu.VMEM((RWKV_W // 128, HEAD_DIM, 128), F32),
                        pltpu.VMEM((tt, RWKV_W), F32)],
        compiler_params=_cparams(("arbitrary", "arbitrary")),
        name="rwkv7",
    )(rkv, lora, lora_w, vecs, ones, tri)


def _mix_kernel(x_ref, o0, o1, o2, l0, l1, l2, orw_ref, gate_ref,
                wau_ref, wru_ref, wout_ref, gpost_ref, h_ref):
    full = lambda ref: jnp.concatenate([ref[c] for c in range(ref.shape[0])], axis=1)
    ls = [full(l0), full(l1), full(l2)]
    m = jnp.maximum(jnp.maximum(ls[0], ls[1]), ls[2])
    ws = [jnp.exp(l - m) for l in ls]
    o_attn = (ws[0] * full(o0) + ws[1] * full(o1) + ws[2] * full(o2)) / (ws[0] + ws[1] + ws[2])
    d = x_ref.shape[1]
    ga = gate_ref[:, 0:d].astype(F32)
    gr = gate_ref[:, d:2 * d].astype(F32)
    merged = ga * _dot(o_attn.astype(BF16), wau_ref[...]) + gr * _dot(orw_ref[...], wru_ref[...])
    mo = _dot(merged.astype(BF16), wout_ref[...])
    h_ref[...] = x_ref[...] + _rms(mo, gpost_ref[...])


def _mix(x2, os_, ls_, orw, gates, wau, wru, wout, gpost, tm):
    T, D = x2.shape
    row = lambda w: pl.BlockSpec((tm, w), lambda i: (i, 0))
    per_b = os_[0].shape[2] // tm
    attn = pl.BlockSpec((None, ATTN_W // 128, tm, 128), lambda i: (i // per_b, 0, i % per_b, 0))
    return pl.pallas_call(
        _mix_kernel,
        grid=(T // tm,),
        in_specs=[row(D)] + [attn] * 6 + [row(RWKV_W), row(2 * D),
                  _const_spec(wau.shape), _const_spec(wru.shape), _const_spec(wout.shape),
                  _const_spec(gpost.shape)],
        out_specs=row(D),
        out_shape=jax.ShapeDtypeStruct((T, D), F32),
        compiler_params=_cparams(("arbitrary",)),
        name="mix",
    )(x2, *os_, *ls_, orw, gates, wau, wru, wout, gpost)


def _ffn_kernel(h_ref, p_ref, gpre_ref, wfi_ref, wfo_ref, gpost_ref, wple_ref, wpg_ref,
                gple_ref, out_ref):
    h = h_ref[...]
    f = _rms(h, gpre_ref[...]).astype(BF16)
    dff = wfi_ref.shape[1]
    nck = 4
    ck = dff // nck
    acc = jnp.zeros(h.shape, F32)
    for j in range(nck):
        a = jnp.maximum(_dot(f, wfi_ref[:, j * ck:(j + 1) * ck]), 0.0)
        acc = acc + _dot((a * a).astype(BF16), wfo_ref[j * ck:(j + 1) * ck, :])
    h = h + _rms(acc, gpost_ref[...])
    e = _dot(p_ref[...].astype(BF16), wple_ref[...]) * jax.nn.sigmoid(
        _dot(h.astype(BF16), wpg_ref[...]))
    out_ref[...] = h + _rms(e, gple_ref[...])


def _ffn(h, p2, gpre, wfi, wfo, gpost, wple, wpg, gple, tm):
    T, D = h.shape
    row = lambda w: pl.BlockSpec((tm, w), lambda i: (i, 0))
    return pl.pallas_call(
        _ffn_kernel,
        grid=(T // tm,),
        in_specs=[row(D), row(p2.shape[1]), _const_spec(gpre.shape), _const_spec(wfi.shape),
                  _const_spec(wfo.shape), _const_spec(gpost.shape), _const_spec(wple.shape),
                  _const_spec(wpg.shape), _const_spec(gple.shape)],
        out_specs=row(D),
        out_shape=jax.ShapeDtypeStruct((T, D), F32),
        compiler_params=_cparams(("arbitrary",)),
        name="ffn_ple",
    )(h, p2, gpre, wfi, wfo, gpost, wple, wpg, gple)


def _rope_tables(S):
    half = HEAD_DIM // 2
    inv_freq = ROPE_THETA ** (-jnp.arange(0, HEAD_DIM, 2, dtype=F32) / HEAD_DIM)
    ang = jnp.arange(S, dtype=F32)[:, None] * inv_freq[None, :]
    c, s = jnp.cos(ang), jnp.sin(ang)
    cos_t = jnp.tile(jnp.concatenate([c, c], axis=1), (1, 2))
    sin_t = jnp.tile(jnp.concatenate([-s, s], axis=1), (1, 2))
    scale = HEAD_DIM ** -0.5
    return jnp.concatenate([cos_t * scale, sin_t * scale, cos_t, sin_t], axis=1)


def _pad_slots(parts, axis):
    out = []
    for a in parts:
        pad = [(0, 0)] * a.ndim
        pad[axis] = (0, LORA_SLOT - a.shape[axis])
        out.append(jnp.pad(a, pad))
    return jnp.concatenate(out, axis=axis)


def kernel(x, p, mix_pre_norm, w_in, rwkv_mu, rwkv_w0, rwkv_w2, rwkv_a0, rwkv_a2, rwkv_g2,
           rwkv_k_k, rwkv_k_a, rwkv_r_k, rwkv_ln_w, rwkv_ln_b, w_attn_up, w_rwkv_up, w_out,
           mix_post_norm, ffn_pre_norm, w_ff_in, w_ff_out, ffn_post_norm, w_ple, w_ple_gate,
           ple_post_norm):
    B, S, D = x.shape
    depth = w_in.shape[0]
    nqk = 2 * N_GROUPS * ATTN_W
    nqkv = 3 * N_GROUPS * ATTN_W
    nr = 3 * RWKV_W
    lo = nqkv + nr
    h = x
    rope = _rope_tables(S)
    for i in range(depth):
        w = w_in[i]
        wl = _pad_slots([w[:, lo:lo + DECAY_LORA],
                         w[:, lo + DECAY_LORA:lo + DECAY_LORA + ICLR_LORA],
                         w[:, lo + DECAY_LORA + ICLR_LORA:lo + DECAY_LORA + ICLR_LORA + GATE_LORA]], 1)
        ng = lo + DECAY_LORA + ICLR_LORA + GATE_LORA
        mu = rwkv_mu[i]
        mul = _pad_slots([mu[nr:nr + DECAY_LORA], mu[nr + DECAY_LORA:nr + DECAY_LORA + ICLR_LORA],
                          mu[nr + DECAY_LORA + ICLR_LORA:]], 0)
        wqkv = w[:, :nqkv].reshape(D, 3, N_GROUPS, ATTN_W).transpose(0, 2, 1, 3).reshape(D, nqkv)
        q0, q1, q2, rkv, lora, gates = _in_proj(
            h, mix_pre_norm[i][None, :], wqkv.astype(BF16),
            w[:, nqkv:lo].astype(BF16), wl.astype(BF16), w[:, ng:].astype(BF16), rope,
            mu[None, :nr], mul[None, :], tm=min(256, S))

        os_, ls_ = [], []
        for qkv_g, (_, dil) in zip((q0, q1, q2), ATTN_GROUPS):
            o, lse = _attn_group(qkv_g, dil, tok=min(2048, S))
            os_.append(o)
            ls_.append(lse)

        lora_w = _pad_slots([rwkv_w2[i], rwkv_a2[i], rwkv_g2[i]], 0)
        vecs = jnp.stack([rwkv_w0[i], rwkv_a0[i], rwkv_k_k[i], rwkv_k_a[i],
                          rwkv_r_k[i].reshape(-1), rwkv_ln_w[i], rwkv_ln_b[i],
                          jnp.zeros((RWKV_W,), F32)])
        orw = _rwkv(rkv, lora, lora_w, vecs, tt=min(256, S))

        tm = min(512, S)
        h1 = _mix(h.reshape(B * S, D), os_, ls_, orw.reshape(B * S, RWKV_W),
                  gates.reshape(B * S, 2 * D), w_attn_up[i].astype(BF16),
                  w_rwkv_up[i].astype(BF16), w_out[i].astype(BF16), mix_post_norm[i][None, :], tm)
        h2 = _ffn(h1, p[i].reshape(B * S, -1), ffn_pre_norm[i][None, :], w_ff_in[i].astype(BF16),
                  w_ff_out[i].astype(BF16), ffn_post_norm[i][None, :], w_ple[i].astype(BF16),
                  w_ple_gate[i].astype(BF16), ple_post_norm[i][None, :], tm)
        h = h2.reshape(B, S, D)
    return h
```

```python
import functools

import numpy as np
import jax
import jax.numpy as jnp
from jax import lax
from jax.experimental import pallas as pl
from jax.experimental.pallas import tpu as pltpu

F32 = jnp.float32
BF16 = jnp.bfloat16

HEAD_DIM = 64
ATTN_GROUPS = ((128, 1), (512, 4), (2048, 16))
N_GROUPS = 3
ATTN_W = 256
ATTN_KEYS = 128
ROPE_THETA = 10000.0
RWKV_W = 512
RWKV_HEADS = 8
DECAY_LORA, ICLR_LORA, GATE_LORA = 32, 32, 96
LORA_SLOT = 128
RWKV_GN_EPS = 64e-5
NORM_EPS = 1e-6
CHUNK = 64
NEG = -1e30

VMEM_LIMIT = 56 * 1024 * 1024
MXU_DIM = 256


def _cparams(sem):
    return pltpu.CompilerParams(dimension_semantics=sem, vmem_limit_bytes=VMEM_LIMIT)


def _const_spec(shape):
    nd = len(shape)
    return pl.BlockSpec(shape, lambda *_: (0,) * nd, pipeline_mode=pl.Buffered(1))


def _dot(a, b):
    return jnp.dot(a, b, preferred_element_type=F32)


def _dot_nt(a, b):
    return lax.dot_general(a, b, (((1,), (1,)), ((), ())), preferred_element_type=F32)


def _dot_tn(a, b):
    return lax.dot_general(a, b, (((0,), (0,)), ((), ())), preferred_element_type=F32)


def _split3(x):
    hi = x.astype(BF16)
    r1 = x - hi.astype(F32)
    mid = r1.astype(BF16)
    lo = (r1 - mid.astype(F32)).astype(BF16)
    return hi, mid, lo


def _dot_exact_rhs(x, rhs_bf16, passes):
    parts = _split3(x)[:passes]
    acc = _dot(parts[0], rhs_bf16)
    for p in parts[1:]:
        acc = acc + _dot(p, rhs_bf16)
    return acc


def _exact_lhs_dot(lhs_bf16, x, passes):
    parts = _split3(x)[:passes]
    acc = _dot(lhs_bf16, parts[0])
    for p in parts[1:]:
        acc = acc + _dot(lhs_bf16, p)
    return acc


def _rms(x, g):
    ms = jnp.mean(x * x, axis=-1, keepdims=True)
    return x * lax.rsqrt(ms + NORM_EPS) * g


def _in_proj_kernel(x_ref, g_ref, wqkv_ref, wr_ref, wl_ref, wg_ref, rope_ref,
                    mur_ref, mul_ref, qkv0_ref, qkv1_ref, qkv2_ref, rkv_ref, lora_ref, gate_ref,
                    carry_r, carry_l, qkv_scr):
    s = pl.program_id(1)
    tm = x_ref.shape[0]
    u = _rms(x_ref[...], g_ref[...]).astype(BF16)

    z = _dot(u, wqkv_ref[...])
    lane = lax.broadcasted_iota(jnp.int32, (1, 128), 1)
    first_half = (lane % HEAD_DIM) < (HEAD_DIM // 2)
    gw = 3 * ATTN_W
    per_group = gw // 128
    for c in range(z.shape[1] // 128):
        g, j = divmod(c, per_group)
        xc = z[:, c * 128:(c + 1) * 128]
        if j < 2 * (ATTN_W // 128):
            t0 = 0 if j < ATTN_W // 128 else 256
            rot = jnp.where(first_half, pltpu.roll(xc, 96, 1), pltpu.roll(xc, 32, 1))
            xc = xc * rope_ref[:, t0:t0 + 128] + rot * rope_ref[:, t0 + 128:t0 + 256]
        if g == 0:
            qkv0_ref[0, :, c * 128:(c + 1) * 128] = xc.astype(BF16)
        else:
            qkv_scr[c - per_group] = xc
    for g, out_ref in ((1, qkv1_ref), (2, qkv2_ref)):
        dil = out_ref.shape[0]
        for r in range(dil):
            for cb in range(per_group):
                out_ref[r, :, cb * 128:(cb + 1) * 128] = qkv_scr[
                    (g - 1) * per_group + cb, pl.ds(r, tm // dil, stride=dil), :].astype(BF16)

    row0 = lax.broadcasted_iota(jnp.int32, (tm, 1), 0) == 0

    @pl.when(s == 0)
    def _():
        carry_r[...] = jnp.zeros_like(carry_r)
        carry_l[...] = jnp.zeros_like(carry_l)

    def shift_lerp(z, carry, mu):
        zs = jnp.where(row0, carry[0:1, :], pltpu.roll(z, 1, 0))
        carry[0:1, :] = z[tm - 1:tm, :]
        return z + (zs - z) * mu

    rkv_ref[...] = shift_lerp(_dot(u, wr_ref[...]), carry_r, mur_ref[...])
    lora_ref[...] = shift_lerp(_dot(u, wl_ref[...]), carry_l, mul_ref[...])

    gate_ref[...] = jax.nn.sigmoid(_dot(u, wg_ref[...])).astype(BF16)


def _in_proj(x, g, wqkv, wr, wl, wg, rope, mur, mul, tm):
    B, S, D = x.shape
    gw = 3 * ATTN_W
    row = lambda w: pl.BlockSpec((None, tm, w), lambda b, s: (b, s, 0))
    dils = [d for _, d in ATTN_GROUPS]
    qkv_specs = [pl.BlockSpec((None, d, tm // d, gw), lambda b, s: (b, 0, s, 0)) for d in dils]
    qkv_shapes = [jax.ShapeDtypeStruct((B, d, S // d, gw), BF16) for d in dils]
    return pl.pallas_call(
        _in_proj_kernel,
        grid=(B, S // tm),
        in_specs=[row(D), _const_spec(g.shape), _const_spec(wqkv.shape),
                  _const_spec(wr.shape), _const_spec(wl.shape), _const_spec(wg.shape),
                  pl.BlockSpec((tm, rope.shape[1]), lambda b, s: (s, 0)),
                  _const_spec(mur.shape), _const_spec(mul.shape)],
        out_specs=qkv_specs + [row(wr.shape[1]), row(wl.shape[1]), row(wg.shape[1])],
        out_shape=qkv_shapes + [jax.ShapeDtypeStruct((B, S, wr.shape[1]), F32),
                                jax.ShapeDtypeStruct((B, S, wl.shape[1]), F32),
                                jax.ShapeDtypeStruct((B, S, wg.shape[1]), BF16)],
        scratch_shapes=[pltpu.VMEM((8, wr.shape[1]), F32), pltpu.VMEM((8, wl.shape[1]), F32),
                        pltpu.VMEM(((N_GROUPS - 1) * gw // 128, tm, 128), F32)],
        compiler_params=_cparams(("arbitrary", "arbitrary")),
        name="in_proj",
    )(x, g, wqkv, wr, wl, wg, rope, mur, mul)


def _attn_kernel(cur_ref, kp_ref, vp_ref, o_ref, lse_ref, *, dil, nsub):
    i = pl.program_id(1)
    kw = 2 * ATTN_KEYS
    col = lax.broadcasted_iota(jnp.int32, (ATTN_KEYS, kw), 1)
    rowi = lax.broadcasted_iota(jnp.int32, (ATTN_KEYS, kw), 0)
    diff = col - rowi
    band = (diff >= 0) & (diff <= ATTN_KEYS)
    lane = lax.broadcasted_iota(jnp.int32, (1, ATTN_W), 1)
    hmask = [(lane // HEAD_DIM) == h for h in range(ATTN_W // HEAD_DIM)]
    zero_q = jnp.zeros((ATTN_KEYS, ATTN_W), BF16)
    qs, ks, vs = slice(0, ATTN_W), slice(ATTN_W, 2 * ATTN_W), slice(2 * ATTN_W, 3 * ATTN_W)

    def load(u):
        r, sb = (u, 0) if nsub == 1 else ((0, u) if dil == 1 else (u // nsub, u % nsub))
        rows = pl.ds(pl.multiple_of(sb * ATTN_KEYS, ATTN_KEYS), ATTN_KEYS)
        q = cur_ref[r, rows, qs]
        if nsub == 1:
            k_lo, v_lo = kp_ref[r], vp_ref[r]
        else:
            lo = pl.ds(pl.multiple_of(jnp.maximum(sb - 1, 0) * ATTN_KEYS, ATTN_KEYS), ATTN_KEYS)
            first = sb == 0
            k_lo = jnp.where(first, kp_ref[r], cur_ref[r, lo, ks])
            v_lo = jnp.where(first, vp_ref[r], cur_ref[r, lo, vs])
        k = jnp.concatenate([k_lo, cur_ref[r, rows, ks]], axis=0)
        v = jnp.concatenate([v_lo, cur_ref[r, rows, vs]], axis=0)
        valid = band & ((col >= ATTN_KEYS) | (i > 0) | (sb > 0))
        return r, sb, q, k, v, valid

    def body(j, carry):
        us = [load(2 * j), load(2 * j + 1)]
        scs = [[jnp.where(valid, _dot_nt(jnp.where(hm, q, zero_q), k), NEG) for hm in hmask]
               for (_, _, q, k, _, valid) in us]
        ms = [[jnp.max(sc, axis=-1, keepdims=True) for sc in row] for row in scs]
        es = [[jnp.exp(sc - m) for sc, m in zip(r1, r2)] for r1, r2 in zip(scs, ms)]
        dens = [[jnp.sum(e, axis=-1, keepdims=True) for e in row] for row in es]
        ohs = [[_dot(e.astype(BF16), u_[4]) for e in row] for row, u_ in zip(es, us)]
        for (r, sb, *_), oh_r, den_r, m_r in zip(us, ohs, dens, ms):
            o_acc = jnp.zeros((ATTN_KEYS, ATTN_W), F32)
            l_acc = jnp.zeros((ATTN_KEYS, ATTN_W), F32)
            for hm, oh, den, m in zip(hmask, oh_r, den_r, m_r):
                o_acc = jnp.where(hm, oh / den, o_acc)
                l_acc = jnp.where(hm, m + jnp.log(den), l_acc)
            if dil == 1:
                dst = pl.ds(pl.multiple_of(sb * ATTN_KEYS, ATTN_KEYS), ATTN_KEYS)
            else:
                dst = pl.ds(r + sb * (ATTN_KEYS * dil), ATTN_KEYS, stride=dil)
            for cb in range(ATTN_W // 128):
                o_ref[cb, dst, :] = o_acc[:, cb * 128:(cb + 1) * 128]
                lse_ref[cb, dst, :] = l_acc[:, cb * 128:(cb + 1) * 128]
        return carry

    lax.fori_loop(0, dil * nsub // 2, body, 0)


def _attn_group(qkv_g, dil, tok):
    B, _, n, gw = qkv_g.shape
    S = n * dil
    tq = tok // dil
    nsub = tq // ATTN_KEYS
    cur = pl.BlockSpec((None, dil, tq, gw), lambda b, i: (b, 0, i, 0))
    prev = lambda c: pl.BlockSpec((None, dil, ATTN_KEYS, ATTN_W),
                                  lambda b, i: (b, 0, jnp.maximum(i * nsub - 1, 0), c))
    out = pl.BlockSpec((None, ATTN_W // 128, tok, 128), lambda b, i: (b, 0, i, 0))
    return pl.pallas_call(
        functools.partial(_attn_kernel, dil=dil, nsub=nsub),
        grid=(B, S // tok),
        in_specs=[cur, prev(1), prev(2)],
        out_specs=[out, out],
        out_shape=[jax.ShapeDtypeStruct((B, ATTN_W // 128, S, 128), F32)] * 2,
        compiler_params=_cparams(("arbitrary", "arbitrary")),
        name=f"attn_d{dil}",
    )(qkv_g, qkv_g, qkv_g)


def _bd(x):
    lane = lax.broadcasted_iota(jnp.int32, (1, x.shape[1]), 1)
    m0 = (lane % 128) < HEAD_DIM
    z = jnp.zeros_like(x)
    return jnp.concatenate([jnp.where(m0, x, z), jnp.where(m0, z, x)], axis=0)


def _rwkv_kernel(rkv_ref, lora_ref, lw_ref, vec_ref, ones_ref, tri_ref,
                 o_ref, st_ref, y_scr):
    s = pl.program_id(1)
    tt = rkv_ref.shape[0]
    nchunk = tt // CHUNK
    npair = RWKV_W // 128

    @pl.when(s == 0)
    def _():
        st_ref[...] = jnp.zeros_like(st_ref)

    w0, a0, k_k, k_a, r_k, ln_w, ln_b = [vec_ref[j:j + 1, :] for j in range(7)]
    r = rkv_ref[:, 0:RWKV_W]
    k = rkv_ref[:, RWKV_W:2 * RWKV_W]
    v = rkv_ref[:, 2 * RWKV_W:3 * RWKV_W]
    xw = lora_ref[:, 0:LORA_SLOT]
    xa = lora_ref[:, LORA_SLOT:2 * LORA_SLOT]
    xg = lora_ref[:, 2 * LORA_SLOT:3 * LORA_SLOT]
    ones = ones_ref[...]
    ow = ones.shape[0]

    def head_sum(x, passes):
        return jnp.concatenate([_dot_exact_rhs(x[:, c:c + ow], ones, passes)
                                for c in range(0, RWKV_W, ow)], axis=1)

    def lora_dot(x, j):
        return _dot(x.astype(BF16), lw_ref[j * LORA_SLOT:(j + 1) * LORA_SLOT, :])

    y_w = -(w0 + lora_dot(jnp.tanh(xw), 0))
    softplus = jnp.maximum(y_w, 0.0) + jnp.log(1.0 + jnp.exp(-jnp.abs(y_w)))
    ld = -jnp.exp(-softplus - 0.5)
    a = jax.nn.sigmoid(a0 + lora_dot(xa, 1))
    g = lora_dot(jax.nn.sigmoid(xg), 2)
    kk = k * k_k
    kk = kk / jnp.maximum(jnp.sqrt(head_sum(kk * kk, 1)), 1e-12)
    kp = k * (1.0 + (a - 1.0) * k_a)
    avec = -kk
    bvec = kk * a

    cum = _exact_lhs_dot(tri_ref[...], ld, 2)

    lane = lax.broadcasted_iota(jnp.int32, (1, 128), 1)
    m0 = lane < HEAD_DIM
    ri = lax.broadcasted_iota(jnp.int32, (CHUNK, 128), 0)
    ci = lax.broadcasted_iota(jnp.int32, (CHUNK, 128), 1) % HEAD_DIM
    strict = ci < ri
    incl = ci <= ri
    eye = ci == ri
    blk16 = (ci // 16) == (ri // 16)
    blk32 = (ci // 32) == (ri // 32)

    units = [(slice(c * CHUNK, (c + 1) * CHUNK), slice(p * 128, (p + 1) * 128))
             for c in range(nchunk) for p in range(npair)]
    each = lambda fn, *lists: [fn(*args) for args in zip(*lists)]
    cat0 = lambda *xs: jnp.concatenate(xs, axis=0)
    cat1 = lambda *xs: jnp.concatenate(xs, axis=1)

    def prep(u):
        rows, cols = u
        cm = cum[rows, cols]
        c_last = cm[CHUNK - 1:CHUNK, :]
        w_in = jnp.exp(cm)
        w_ex = jnp.exp(cm - ld[rows, cols])
        w_inv = jnp.exp(-cm)
        w_out = jnp.exp(c_last - cm)
        r_f = r[rows, cols] * w_in
        return dict(
            r_f=r_f, r_t=r_f.astype(BF16),
            a_t=(avec[rows, cols] * w_ex).astype(BF16),
            b_t=(bvec[rows, cols] * w_inv).astype(BF16),
            k_t=(kp[rows, cols] * w_inv).astype(BF16),
            b_o=(bvec[rows, cols] * w_out).astype(BF16),
            k_o=(kp[rows, cols] * w_out).astype(BF16),
            vv=v[rows, cols].astype(BF16),
            wc=jnp.where(eye, jnp.exp(c_last), 0.0))

    P = each(prep, units)
    get = lambda key: [d[key] for d in P]

    sc = each(lambda d: _dot_nt(cat0(d["a_t"], d["r_t"]), cat0(_bd(d["b_t"]), _bd(d["k_t"]))), P)
    l_ab = [jnp.where(strict, x[0:CHUNK, 0:128], 0.0) for x in sc]
    m_ak = [jnp.where(strict, x[0:CHUNK, 128:256], 0.0) for x in sc]
    m_rb = [jnp.where(incl, x[CHUNK:, 0:128], 0.0).astype(BF16) for x in sc]
    m_rk = [jnp.where(incl, x[CHUNK:, 128:256], 0.0) for x in sc]

    mv = each(lambda ak, rk, d: _dot(cat0(ak, rk).astype(BF16), _bd(d["vv"])), m_ak, m_rk, P)
    makv = [x[0:CHUNK].astype(BF16) for x in mv]
    mrkv = [x[CHUNK:] for x in mv]

    pw = [jnp.where(blk16, x, 0.0).astype(BF16) for x in l_ab]
    tmat = [jnp.where(eye, 1.0, 0.0) for _ in units]
    for it in range(4):
        if it < 3:
            z = each(lambda a, t: _dot(a, _bd(cat1(a, t.astype(BF16)))), pw, tmat)
            pw = [x[:, 0:128].astype(BF16) for x in z]
            tmat = each(lambda t, x: t + x[:, 128:256], tmat, z)
        else:
            tmat = each(lambda a, t: t + _dot(a, _bd(t.astype(BF16))), pw, tmat)
    for sel in (lambda x: jnp.where(blk32 & ~blk16, x, 0.0), lambda x: jnp.where(blk32, 0.0, x)):
        off = [sel(x).astype(BF16) for x in l_ab]
        t16 = [t.astype(BF16) for t in tmat]
        ot = each(lambda o, t: _dot(o, _bd(t)).astype(BF16), off, t16)
        tmat = each(lambda t, t6, o: t + _dot(t6, _bd(o)), tmat, t16, ot)

    xh = each(lambda t, d, m: _dot(t.astype(BF16), _bd(cat1(d["a_t"], m))), tmat, P, makv)
    ahv = [x.astype(BF16) for x in xh]

    mr = each(lambda m, x: _dot(m, _bd(x)), m_rb, ahv)
    r_hat = each(lambda d, x: d["r_f"] + x[:, 0:128], P, mr)
    y_intra = each(lambda x, y: x[:, 128:256] + y, mr, mrkv)

    gh = each(lambda d, x: _dot_tn(cat0(d["b_o"], d["k_o"]),
                                   cat0(x, cat1(jnp.zeros_like(d["vv"]), d["vv"]))), P, ahv)
    gt = each(lambda x, d: jnp.where(m0, x[0:CHUNK, 0:128], x[CHUNK:, 0:128]) + d["wc"], gh, P)
    ht = [jnp.where(m0, x[0:CHUNK, 128:256], x[CHUNK:, 128:256]) for x in gh]
    lhs = each(lambda a, b: cat0(a, b).astype(BF16), r_hat, gt)

    st = [st_ref[p] for p in range(npair)]
    for c in range(nchunk):
        sl = slice(c * npair, (c + 1) * npair)
        ys = each(lambda a, s_: _dot(a, _bd(s_.astype(BF16))), lhs[sl], st)
        for (rows, cols), y_, yi in zip(units[sl], ys, y_intra[sl]):
            y_scr[rows, cols] = y_[0:CHUNK] + yi
        st = each(lambda y_, h_: y_[CHUNK:] + h_, ys, ht[sl])
    for p in range(npair):
        st_ref[p] = st[p]


    y = y_scr[...]
    inv_n = 1.0 / HEAD_DIM
    mean = head_sum(y, 2) * inv_n
    yc = y - mean
    var = head_sum(yc * yc, 1) * inv_n
    yn = yc * lax.rsqrt(var + RWKV_GN_EPS) * ln_w + ln_b
    bonus = head_sum(r * kp * r_k, 1) * v
    o_ref[...] = ((yn + bonus) * g).astype(o_ref.dtype)


def _rwkv(rkv, lora, lora_w, vecs, tt):
    B, S, _ = rkv.shape
    ones = jnp.asarray(np.kron(np.eye(MXU_DIM // HEAD_DIM), np.ones((HEAD_DIM, HEAD_DIM))), BF16)
    tri = jnp.asarray(np.kron(np.eye(tt // CHUNK), np.tril(np.ones((CHUNK, CHUNK)))), BF16)
    row = lambda w: pl.BlockSpec((None, tt, w), lambda b, s: (b, s, 0))
    return pl.pallas_call(
        _rwkv_kernel,
        grid=(B, S // tt),
        in_specs=[row(rkv.shape[2]), row(lora.shape[2]), _const_spec(lora_w.shape),
                  _const_spec(vecs.shape), _const_spec(ones.shape), _const_spec(tri.shape)],
        out_specs=row(RWKV_W),
        out_shape=jax.ShapeDtypeStruct((B, S, RWKV_W), BF16),
        scratch_shapes=[pltpu.VMEM((RWKV_W // 128, HEAD_DIM, 128), F32),
                        pltpu.VMEM((tt, RWKV_W), F32)],
        compiler_params=_cparams(("arbitrary", "arbitrary")),
        name="rwkv7",
    )(rkv, lora, lora_w, vecs, ones, tri)


def _tail_kernel(x_ref, o0, o1, o2, l0, l1, l2, orw_ref, gate_ref, p_ref,
                 wau_ref, wru_ref, wout_ref, gpost_ref, gpre_ref, wfi_ref, wfo_ref, gffn_ref,
                 wple_ref, wpg_ref, gple_ref, out_ref):
    full = lambda ref: jnp.concatenate([ref[c] for c in range(ref.shape[0])], axis=1)
    ls = [full(l0), full(l1), full(l2)]
    m = jnp.maximum(jnp.maximum(ls[0], ls[1]), ls[2])
    ws = [jnp.exp(l - m) for l in ls]
    o_attn = (ws[0] * full(o0) + ws[1] * full(o1) + ws[2] * full(o2)) / (ws[0] + ws[1] + ws[2])
    d = x_ref.shape[1]
    ga = gate_ref[:, 0:d].astype(F32)
    gr = gate_ref[:, d:2 * d].astype(F32)
    merged = ga * _dot(o_attn.astype(BF16), wau_ref[...]) + gr * _dot(orw_ref[...], wru_ref[...])
    mo = _dot(merged.astype(BF16), wout_ref[...])
    h = x_ref[...] + _rms(mo, gpost_ref[...])

    f = _rms(h, gpre_ref[...]).astype(BF16)
    dff = wfi_ref.shape[1]
    nck = 4
    ck = dff // nck
    acc = jnp.zeros(h.shape, F32)
    for j in range(nck):
        a = jnp.maximum(_dot(f, wfi_ref[:, j * ck:(j + 1) * ck]), 0.0)
        acc = acc + _dot((a * a).astype(BF16), wfo_ref[j * ck:(j + 1) * ck, :])
    h = h + _rms(acc, gffn_ref[...])
    e = _dot(p_ref[...].astype(BF16), wple_ref[...]) * jax.nn.sigmoid(
        _dot(h.astype(BF16), wpg_ref[...]))
    out_ref[...] = h + _rms(e, gple_ref[...])


def _tail(x2, os_, ls_, orw, gates, p2, weights, tm):
    T, D = x2.shape
    row = lambda w: pl.BlockSpec((tm, w), lambda i: (i, 0))
    per_b = os_[0].shape[2] // tm
    attn = pl.BlockSpec((None, ATTN_W // 128, tm, 128), lambda i: (i // per_b, 0, i % per_b, 0))
    return pl.pallas_call(
        _tail_kernel,
        grid=(T // tm,),
        in_specs=[row(D)] + [attn] * 6 + [row(RWKV_W), row(2 * D), row(p2.shape[1])]
                 + [_const_spec(w.shape) for w in weights],
        out_specs=row(D),
        out_shape=jax.ShapeDtypeStruct((T, D), F32),
        compiler_params=_cparams(("arbitrary",)),
        name="tail",
    )(x2, *os_, *ls_, orw, gates, p2, *weights)


def _rope_tables(S):
    half = HEAD_DIM // 2
    inv_freq = ROPE_THETA ** (-jnp.arange(0, HEAD_DIM, 2, dtype=F32) / HEAD_DIM)
    ang = jnp.arange(S, dtype=F32)[:, None] * inv_freq[None, :]
    c, s = jnp.cos(ang), jnp.sin(ang)
    cos_t = jnp.tile(jnp.concatenate([c, c], axis=1), (1, 2))
    sin_t = jnp.tile(jnp.concatenate([-s, s], axis=1), (1, 2))
    scale = HEAD_DIM ** -0.5
    return jnp.concatenate([cos_t * scale, sin_t * scale, cos_t, sin_t], axis=1)


def _pad_slots(parts, axis):
    out = []
    for a in parts:
        pad = [(0, 0)] * a.ndim
        pad[axis] = (0, LORA_SLOT - a.shape[axis])
        out.append(jnp.pad(a, pad))
    return jnp.concatenate(out, axis=axis)


def kernel(x, p, mix_pre_norm, w_in, rwkv_mu, rwkv_w0, rwkv_w2, rwkv_a0, rwkv_a2, rwkv_g2,
           rwkv_k_k, rwkv_k_a, rwkv_r_k, rwkv_ln_w, rwkv_ln_b, w_attn_up, w_rwkv_up, w_out,
           mix_post_norm, ffn_pre_norm, w_ff_in, w_ff_out, ffn_post_norm, w_ple, w_ple_gate,
           ple_post_norm):
    B, S, D = x.shape
    depth = w_in.shape[0]
    nqk = 2 * N_GROUPS * ATTN_W
    nqkv = 3 * N_GROUPS * ATTN_W
    nr = 3 * RWKV_W
    lo = nqkv + nr
    h = x
    rope = _rope_tables(S)
    for i in range(depth):
        w = w_in[i]
        wl = _pad_slots([w[:, lo:lo + DECAY_LORA],
                         w[:, lo + DECAY_LORA:lo + DECAY_LORA + ICLR_LORA],
                         w[:, lo + DECAY_LORA + ICLR_LORA:lo + DECAY_LORA + ICLR_LORA + GATE_LORA]], 1)
        ng = lo + DECAY_LORA + ICLR_LORA + GATE_LORA
        mu = rwkv_mu[i]
        mul = _pad_slots([mu[nr:nr + DECAY_LORA], mu[nr + DECAY_LORA:nr + DECAY_LORA + ICLR_LORA],
                          mu[nr + DECAY_LORA + ICLR_LORA:]], 0)
        wqkv = w[:, :nqkv].reshape(D, 3, N_GROUPS, ATTN_W).transpose(0, 2, 1, 3).reshape(D, nqkv)
        q0, q1, q2, rkv, lora, gates = _in_proj(
            h, mix_pre_norm[i][None, :], wqkv.astype(BF16),
            w[:, nqkv:lo].astype(BF16), wl.astype(BF16), w[:, ng:].astype(BF16), rope,
            mu[None, :nr], mul[None, :], tm=min(512, S))

        os_, ls_ = [], []
        for qkv_g, (_, dil) in zip((q0, q1, q2), ATTN_GROUPS):
            o, lse = _attn_group(qkv_g, dil, tok=min(2048, S))
            os_.append(o)
            ls_.append(lse)

        lora_w = _pad_slots([rwkv_w2[i], rwkv_a2[i], rwkv_g2[i]], 0).astype(BF16)
        vecs = jnp.stack([rwkv_w0[i], rwkv_a0[i], rwkv_k_k[i], rwkv_k_a[i],
                          rwkv_r_k[i].reshape(-1), rwkv_ln_w[i], rwkv_ln_b[i],
                          jnp.zeros((RWKV_W,), F32)])
        orw = _rwkv(rkv, lora, lora_w, vecs, tt=min(256, S))

        tm = min(512, S)
        weights = (w_attn_up[i].astype(BF16), w_rwkv_up[i].astype(BF16), w_out[i].astype(BF16),
                   mix_post_norm[i][None, :], ffn_pre_norm[i][None, :], w_ff_in[i].astype(BF16),
                   w_ff_out[i].astype(BF16), ffn_post_norm[i][None, :], w_ple[i].astype(BF16),
                   w_ple_gate[i].astype(BF16), ple_post_norm[i][None, :])
        h2 = _tail(h.reshape(B * S, D), os_, ls_, orw.reshape(B * S, RWKV_W),
                   gates.reshape(B * S, 2 * D), p[i].reshape(B * S, -1), weights, tm)
        h = h2.reshape(B, S, D)
    return h
```

```python
import functools

import numpy as np
import jax
import jax.numpy as jnp
from jax import lax
from jax.experimental import pallas as pl
from jax.experimental.pallas import tpu as pltpu

F32 = jnp.float32
BF16 = jnp.bfloat16

HEAD_DIM = 64
ATTN_GROUPS = ((128, 1), (512, 4), (2048, 16))
N_GROUPS = 3
ATTN_W = 256
ATTN_KEYS = 128
ROPE_THETA = 10000.0
RWKV_W = 512
RWKV_HEADS = 8
DECAY_LORA, ICLR_LORA, GATE_LORA = 32, 32, 96
LORA_SLOT = 128
RWKV_GN_EPS = 64e-5
NORM_EPS = 1e-6
CHUNK = 64
RWKV_SUB = 256
NEG = -1e30

VMEM_LIMIT = 56 * 1024 * 1024
MXU_DIM = 256


def _cparams(sem):
    return pltpu.CompilerParams(dimension_semantics=sem, vmem_limit_bytes=VMEM_LIMIT)


def _const_spec(shape):
    nd = len(shape)
    return pl.BlockSpec(shape, lambda *_: (0,) * nd, pipeline_mode=pl.Buffered(1))


def _dot(a, b):
    return jnp.dot(a, b, preferred_element_type=F32)


def _dot_nt(a, b):
    return lax.dot_general(a, b, (((1,), (1,)), ((), ())), preferred_element_type=F32)


def _dot_tn(a, b):
    return lax.dot_general(a, b, (((0,), (0,)), ((), ())), preferred_element_type=F32)


def _split3(x):
    hi = x.astype(BF16)
    r1 = x - hi.astype(F32)
    mid = r1.astype(BF16)
    lo = (r1 - mid.astype(F32)).astype(BF16)
    return hi, mid, lo


def _dot_exact_rhs(x, rhs_bf16, passes):
    parts = _split3(x)[:passes]
    acc = _dot(parts[0], rhs_bf16)
    for p in parts[1:]:
        acc = acc + _dot(p, rhs_bf16)
    return acc


def _exact_lhs_dot(lhs_bf16, x, passes):
    parts = _split3(x)[:passes]
    acc = _dot(lhs_bf16, parts[0])
    for p in parts[1:]:
        acc = acc + _dot(lhs_bf16, p)
    return acc


def _rms(x, g):
    ms = jnp.mean(x * x, axis=-1, keepdims=True)
    return x * lax.rsqrt(ms + NORM_EPS) * g


def _in_proj_kernel(x_ref, g_ref, wqkv_ref, wr_ref, wl_ref, wg_ref, rope_ref,
                    mur_ref, mul_ref, qkv0_ref, qkv1_ref, qkv2_ref, rkv_ref, lora_ref, gate_ref,
                    carry_r, carry_l, qkv_scr):
    s = pl.program_id(1)
    tm = x_ref.shape[0]
    u = _rms(x_ref[...], g_ref[...]).astype(BF16)

    z = _dot(u, wqkv_ref[...])
    lane = lax.broadcasted_iota(jnp.int32, (1, 128), 1)
    first_half = (lane % HEAD_DIM) < (HEAD_DIM // 2)
    gw = 3 * ATTN_W
    per_group = gw // 128
    for c in range(z.shape[1] // 128):
        g, j = divmod(c, per_group)
        xc = z[:, c * 128:(c + 1) * 128]
        if j < 2 * (ATTN_W // 128):
            t0 = 0 if j < ATTN_W // 128 else 256
            rot = jnp.where(first_half, pltpu.roll(xc, 96, 1), pltpu.roll(xc, 32, 1))
            xc = xc * rope_ref[:, t0:t0 + 128] + rot * rope_ref[:, t0 + 128:t0 + 256]
        if g == 0:
            qkv0_ref[0, :, c * 128:(c + 1) * 128] = xc.astype(BF16)
        else:
            qkv_scr[c - per_group] = xc
    for g, out_ref in ((1, qkv1_ref), (2, qkv2_ref)):
        dil = out_ref.shape[0]
        for r in range(dil):
            for cb in range(per_group):
                out_ref[r, :, cb * 128:(cb + 1) * 128] = qkv_scr[
                    (g - 1) * per_group + cb, pl.ds(r, tm // dil, stride=dil), :].astype(BF16)

    row0 = lax.broadcasted_iota(jnp.int32, (tm, 1), 0) == 0

    @pl.when(s == 0)
    def _():
        carry_r[...] = jnp.zeros_like(carry_r)
        carry_l[...] = jnp.zeros_like(carry_l)

    def shift_lerp(z, carry, mu):
        zs = jnp.where(row0, carry[0:1, :], pltpu.roll(z, 1, 0))
        carry[0:1, :] = z[tm - 1:tm, :]
        return z + (zs - z) * mu

    rkv_ref[...] = shift_lerp(_dot(u, wr_ref[...]), carry_r, mur_ref[...])
    lora_ref[...] = shift_lerp(_dot(u, wl_ref[...]), carry_l, mul_ref[...])

    gate_ref[...] = jax.nn.sigmoid(_dot(u, wg_ref[...])).astype(BF16)


def _in_proj(x, g, wqkv, wr, wl, wg, rope, mur, mul, tm):
    B, S, D = x.shape
    gw = 3 * ATTN_W
    row = lambda w: pl.BlockSpec((None, tm, w), lambda b, s: (b, s, 0))
    dils = [d for _, d in ATTN_GROUPS]
    qkv_specs = [pl.BlockSpec((None, d, tm // d, gw), lambda b, s: (b, 0, s, 0)) for d in dils]
    qkv_shapes = [jax.ShapeDtypeStruct((B, d, S // d, gw), BF16) for d in dils]
    return pl.pallas_call(
        _in_proj_kernel,
        grid=(B, S // tm),
        in_specs=[row(D), _const_spec(g.shape), _const_spec(wqkv.shape),
                  _const_spec(wr.shape), _const_spec(wl.shape), _const_spec(wg.shape),
                  pl.BlockSpec((tm, rope.shape[1]), lambda b, s: (s, 0)),
                  _const_spec(mur.shape), _const_spec(mul.shape)],
        out_specs=qkv_specs + [row(wr.shape[1]), row(wl.shape[1]), row(wg.shape[1])],
        out_shape=qkv_shapes + [jax.ShapeDtypeStruct((B, S, wr.shape[1]), F32),
                                jax.ShapeDtypeStruct((B, S, wl.shape[1]), F32),
                                jax.ShapeDtypeStruct((B, S, wg.shape[1]), BF16)],
        scratch_shapes=[pltpu.VMEM((8, wr.shape[1]), F32), pltpu.VMEM((8, wl.shape[1]), F32),
                        pltpu.VMEM(((N_GROUPS - 1) * gw // 128, tm, 128), F32)],
        compiler_params=_cparams(("arbitrary", "arbitrary")),
        name="in_proj",
    )(x, g, wqkv, wr, wl, wg, rope, mur, mul)


def _attn_kernel(cur_ref, kp_ref, vp_ref, o_ref, lse_ref, *, dil, nsub):
    i = pl.program_id(1)
    kw = 2 * ATTN_KEYS
    col = lax.broadcasted_iota(jnp.int32, (ATTN_KEYS, kw), 1)
    rowi = lax.broadcasted_iota(jnp.int32, (ATTN_KEYS, kw), 0)
    diff = col - rowi
    band = (diff >= 0) & (diff <= ATTN_KEYS)
    lane = lax.broadcasted_iota(jnp.int32, (1, ATTN_W), 1)
    hmask = [(lane // HEAD_DIM) == h for h in range(ATTN_W // HEAD_DIM)]
    zero_q = jnp.zeros((ATTN_KEYS, ATTN_W), BF16)
    qs, ks, vs = slice(0, ATTN_W), slice(ATTN_W, 2 * ATTN_W), slice(2 * ATTN_W, 3 * ATTN_W)

    def load(u):
        r, sb = (u, 0) if nsub == 1 else ((0, u) if dil == 1 else (u // nsub, u % nsub))
        rows = pl.ds(pl.multiple_of(sb * ATTN_KEYS, ATTN_KEYS), ATTN_KEYS)
        q = cur_ref[r, rows, qs]
        if nsub == 1:
            k_lo, v_lo = kp_ref[r], vp_ref[r]
        else:
            lo = pl.ds(pl.multiple_of(jnp.maximum(sb - 1, 0) * ATTN_KEYS, ATTN_KEYS), ATTN_KEYS)
            first = sb == 0
            k_lo = jnp.where(first, kp_ref[r], cur_ref[r, lo, ks])
            v_lo = jnp.where(first, vp_ref[r], cur_ref[r, lo, vs])
        k = jnp.concatenate([k_lo, cur_ref[r, rows, ks]], axis=0)
        v = jnp.concatenate([v_lo, cur_ref[r, rows, vs]], axis=0)
        valid = band & ((col >= ATTN_KEYS) | (i > 0) | (sb > 0))
        return r, sb, q, k, v, valid

    def body(j, carry):
        us = [load(2 * j), load(2 * j + 1)]
        scs = [[jnp.where(valid, _dot_nt(jnp.where(hm, q, zero_q), k), NEG) for hm in hmask]
               for (_, _, q, k, _, valid) in us]
        ms = [[jnp.max(sc, axis=-1, keepdims=True) for sc in row] for row in scs]
        es = [[jnp.exp(sc - m) for sc, m in zip(r1, r2)] for r1, r2 in zip(scs, ms)]
        dens = [[jnp.sum(e, axis=-1, keepdims=True) for e in row] for row in es]
        ohs = [[_dot(e.astype(BF16), u_[4]) for e in row] for row, u_ in zip(es, us)]
        for (r, sb, *_), oh_r, den_r, m_r in zip(us, ohs, dens, ms):
            o_acc = jnp.zeros((ATTN_KEYS, ATTN_W), F32)
            l_acc = jnp.zeros((ATTN_KEYS, ATTN_W), F32)
            for hm, oh, den, m in zip(hmask, oh_r, den_r, m_r):
                o_acc = jnp.where(hm, oh / den, o_acc)
                l_acc = jnp.where(hm, m + jnp.log(den), l_acc)
            if dil == 1:
                dst = pl.ds(pl.multiple_of(sb * ATTN_KEYS, ATTN_KEYS), ATTN_KEYS)
            else:
                dst = pl.ds(r + sb * (ATTN_KEYS * dil), ATTN_KEYS, stride=dil)
            for cb in range(ATTN_W // 128):
                o_ref[cb, dst, :] = o_acc[:, cb * 128:(cb + 1) * 128]
                lse_ref[cb, dst, :] = l_acc[:, cb * 128:(cb + 1) * 128]
        return carry

    lax.fori_loop(0, dil * nsub // 2, body, 0)


def _attn_group(qkv_g, dil, tok):
    B, _, n, gw = qkv_g.shape
    S = n * dil
    tq = tok // dil
    nsub = tq // ATTN_KEYS
    cur = pl.BlockSpec((None, dil, tq, gw), lambda b, i: (b, 0, i, 0))
    prev = lambda c: pl.BlockSpec((None, dil, ATTN_KEYS, ATTN_W),
                                  lambda b, i: (b, 0, jnp.maximum(i * nsub - 1, 0), c))
    out = pl.BlockSpec((None, ATTN_W // 128, tok, 128), lambda b, i: (b, 0, i, 0))
    return pl.pallas_call(
        functools.partial(_attn_kernel, dil=dil, nsub=nsub),
        grid=(B, S // tok),
        in_specs=[cur, prev(1), prev(2)],
        out_specs=[out, out],
        out_shape=[jax.ShapeDtypeStruct((B, ATTN_W // 128, S, 128), F32)] * 2,
        compiler_params=_cparams(("arbitrary", "arbitrary")),
        name=f"attn_d{dil}",
    )(qkv_g, qkv_g, qkv_g)


def _bd(x):
    lane = lax.broadcasted_iota(jnp.int32, (1, x.shape[1]), 1)
    m0 = (lane % 128) < HEAD_DIM
    z = jnp.zeros_like(x)
    return jnp.concatenate([jnp.where(m0, x, z), jnp.where(m0, z, x)], axis=0)


def _rwkv_kernel(rkv_ref, lora_ref, lw_ref, vec_ref, ones_ref, tri_ref,
                 o_ref, st_ref, y_scr):
    s = pl.program_id(1)
    nsub = rkv_ref.shape[0] // RWKV_SUB

    @pl.when(s == 0)
    def _():
        st_ref[...] = jnp.zeros_like(st_ref)

    st_box = [[st_ref[p] for p in range(RWKV_W // 128)]]
    subs = [_rwkv_sub(i * RWKV_SUB, rkv_ref, lora_ref, lw_ref, vec_ref, ones_ref, tri_ref, o_ref,
                      y_scr, st_box) for i in range(nsub)]

    def advance_to(gen, tag):
        while next(gen) != tag:
            pass

    advance_to(subs[0], "prologue done")
    for i in range(nsub):
        main_done, prev_done, next_done = False, i == 0, i + 1 == nsub
        while not (main_done and prev_done and next_done):
            if not main_done:
                main_done = next(subs[i]) == "main done"
            if not prev_done:
                prev_done = next(subs[i - 1], None) is None
            if not next_done:
                next_done = next(subs[i + 1]) == "prologue done"
    for _ in subs[nsub - 1]:
        pass
    for p, st in enumerate(st_box[0]):
        st_ref[p] = st


def _rwkv_sub(base, rkv_ref, lora_ref, lw_ref, vec_ref, ones_ref, tri_ref, o_ref, y_scr, st_box):
    tt = RWKV_SUB
    nchunk = tt // CHUNK
    npair = RWKV_W // 128
    span = slice(base, base + tt)

    w0, a0, k_k, k_a, r_k, ln_w, ln_b = [vec_ref[j:j + 1, :] for j in range(7)]
    r = rkv_ref[span, 0:RWKV_W]
    k = rkv_ref[span, RWKV_W:2 * RWKV_W]
    v = rkv_ref[span, 2 * RWKV_W:3 * RWKV_W]
    xw = lora_ref[span, 0:LORA_SLOT]
    xa = lora_ref[span, LORA_SLOT:2 * LORA_SLOT]
    xg = lora_ref[span, 2 * LORA_SLOT:3 * LORA_SLOT]
    ones = ones_ref[...]
    ow = ones.shape[0]

    def head_sum(x, passes):
        return jnp.concatenate([_dot_exact_rhs(x[:, c:c + ow], ones, passes)
                                for c in range(0, RWKV_W, ow)], axis=1)

    def lora_dot(x, j):
        return _dot(x.astype(BF16), lw_ref[j * LORA_SLOT:(j + 1) * LORA_SLOT, :])

    y_w = -(w0 + lora_dot(jnp.tanh(xw), 0))
    softplus = jnp.maximum(y_w, 0.0) + jnp.log(1.0 + jnp.exp(-jnp.abs(y_w)))
    ld = -jnp.exp(-softplus - 0.5)
    yield "prologue"
    cum = _exact_lhs_dot(tri_ref[...], ld, 2)
    yield "prologue"
    a = jax.nn.sigmoid(a0 + lora_dot(xa, 1))
    g = lora_dot(jax.nn.sigmoid(xg), 2)
    yield "prologue"
    kk = k * k_k
    kk = kk / jnp.maximum(jnp.sqrt(head_sum(kk * kk, 1)), 1e-12)
    yield "prologue"
    kp = k * (1.0 + (a - 1.0) * k_a)
    avec = -kk
    bvec = kk * a
    yield "prologue"

    lane = lax.broadcasted_iota(jnp.int32, (1, 128), 1)
    m0 = lane < HEAD_DIM
    ri = lax.broadcasted_iota(jnp.int32, (CHUNK, 128), 0)
    ci = lax.broadcasted_iota(jnp.int32, (CHUNK, 128), 1) % HEAD_DIM
    strict = ci < ri
    incl = ci <= ri
    eye = ci == ri
    blk16 = (ci // 16) == (ri // 16)
    blk32 = (ci // 32) == (ri // 32)

    units = [(slice(c * CHUNK, (c + 1) * CHUNK), slice(p * 128, (p + 1) * 128))
             for c in range(nchunk) for p in range(npair)]
    each = lambda fn, *lists: [fn(*args) for args in zip(*lists)]
    cat0 = lambda *xs: jnp.concatenate(xs, axis=0)
    cat1 = lambda *xs: jnp.concatenate(xs, axis=1)

    def prep(u):
        rows, cols = u
        cm = cum[rows, cols]
        c_last = cm[CHUNK - 1:CHUNK, :]
        w_in = jnp.exp(cm)
        w_ex = jnp.exp(cm - ld[rows, cols])
        w_inv = jnp.exp(-cm)
        w_out = jnp.exp(c_last - cm)
        r_f = r[rows, cols] * w_in
        return dict(
            r_f=r_f, r_t=r_f.astype(BF16),
            a_t=(avec[rows, cols] * w_ex).astype(BF16),
            b_t=(bvec[rows, cols] * w_inv).astype(BF16),
            k_t=(kp[rows, cols] * w_inv).astype(BF16),
            b_o=(bvec[rows, cols] * w_out).astype(BF16),
            k_o=(kp[rows, cols] * w_out).astype(BF16),
            vv=v[rows, cols].astype(BF16),
            wc=jnp.where(eye, jnp.exp(c_last), 0.0))

    P = []
    for c in range(nchunk):
        P += each(prep, units[c * npair:(c + 1) * npair])
        yield "prologue"
    yield "prologue done"

    sc = each(lambda d: _dot_nt(cat0(d["a_t"], d["r_t"]), cat0(_bd(d["b_t"]), _bd(d["k_t"]))), P)
    yield "main"
    l_ab = [jnp.where(strict, x[0:CHUNK, 0:128], 0.0) for x in sc]
    m_ak = [jnp.where(strict, x[0:CHUNK, 128:256], 0.0) for x in sc]
    m_rb = [jnp.where(incl, x[CHUNK:, 0:128], 0.0).astype(BF16) for x in sc]
    m_rk = [jnp.where(incl, x[CHUNK:, 128:256], 0.0) for x in sc]

    mv = each(lambda ak, rk, d: _dot(cat0(ak, rk).astype(BF16), _bd(d["vv"])), m_ak, m_rk, P)
    yield "main"
    makv = [x[0:CHUNK].astype(BF16) for x in mv]
    mrkv = [x[CHUNK:] for x in mv]

    pw = [jnp.where(blk16, x, 0.0).astype(BF16) for x in l_ab]
    tmat = [jnp.where(eye, 1.0, 0.0) for _ in units]
    for it in range(4):
        if it < 3:
            z = each(lambda a, t: _dot(a, _bd(cat1(a, t.astype(BF16)))), pw, tmat)
            pw = [x[:, 0:128].astype(BF16) for x in z]
            tmat = each(lambda t, x: t + x[:, 128:256], tmat, z)
        else:
            tmat = each(lambda a, t: t + _dot(a, _bd(t.astype(BF16))), pw, tmat)
        yield "main"
    for sel in (lambda x: jnp.where(blk32 & ~blk16, x, 0.0), lambda x: jnp.where(blk32, 0.0, x)):
        off = [sel(x).astype(BF16) for x in l_ab]
        t16 = [t.astype(BF16) for t in tmat]
        ot = each(lambda o, t: _dot(o, _bd(t)).astype(BF16), off, t16)
        yield "main"
        tmat = each(lambda t, t6, o: t + _dot(t6, _bd(o)), tmat, t16, ot)
        yield "main"

    xh = each(lambda t, d, m: _dot(t.astype(BF16), _bd(cat1(d["a_t"], m))), tmat, P, makv)
    yield "main"
    ahv = [x.astype(BF16) for x in xh]

    mr = each(lambda m, x: _dot(m, _bd(x)), m_rb, ahv)
    yield "main"
    r_hat = each(lambda d, x: d["r_f"] + x[:, 0:128], P, mr)
    y_intra = each(lambda x, y: x[:, 128:256] + y, mr, mrkv)

    gh = each(lambda d, x: _dot_tn(cat0(d["b_o"], d["k_o"]),
                                   cat0(x, cat1(jnp.zeros_like(d["vv"]), d["vv"]))), P, ahv)
    gt = each(lambda x, d: jnp.where(m0, x[0:CHUNK, 0:128], x[CHUNK:, 0:128]) + d["wc"], gh, P)
    ht = [jnp.where(m0, x[0:CHUNK, 128:256], x[CHUNK:, 128:256]) for x in gh]
    lhs = each(lambda a, b: cat0(a, b).astype(BF16), r_hat, gt)
    yield "main done"

    st = st_box[0]
    for c in range(nchunk):
        sl = slice(c * npair, (c + 1) * npair)
        ys = each(lambda a, s_: _dot(a, _bd(s_.astype(BF16))), lhs[sl], st)
        for (rows, cols), y_, yi in zip(units[sl], ys, y_intra[sl]):
            y_scr[base + rows.start:base + rows.stop, cols] = y_[0:CHUNK] + yi
        st = each(lambda y_, h_: y_[CHUNK:] + h_, ys, ht[sl])
        yield "state pass"
    st_box[0] = st

    y = y_scr[span, :]
    inv_n = 1.0 / HEAD_DIM
    mean = head_sum(y, 2) * inv_n
    yield "epilogue"
    yc = y - mean
    var = head_sum(yc * yc, 1) * inv_n
    yield "epilogue"
    yn = yc * lax.rsqrt(var + RWKV_GN_EPS) * ln_w + ln_b
    bonus = head_sum(r * kp * r_k, 1) * v
    o_ref[span, :] = ((yn + bonus) * g).astype(o_ref.dtype)


def _rwkv(rkv, lora, lora_w, vecs, tt):
    B, S, _ = rkv.shape
    ones = jnp.asarray(np.kron(np.eye(MXU_DIM // HEAD_DIM), np.ones((HEAD_DIM, HEAD_DIM))), BF16)
    tri = jnp.asarray(np.kron(np.eye(RWKV_SUB // CHUNK), np.tril(np.ones((CHUNK, CHUNK)))), BF16)
    row = lambda w: pl.BlockSpec((None, tt, w), lambda b, s: (b, s, 0))
    return pl.pallas_call(
        _rwkv_kernel,
        grid=(B, S // tt),
        in_specs=[row(rkv.shape[2]), row(lora.shape[2]), _const_spec(lora_w.shape),
                  _const_spec(vecs.shape), _const_spec(ones.shape), _const_spec(tri.shape)],
        out_specs=row(RWKV_W),
        out_shape=jax.ShapeDtypeStruct((B, S, RWKV_W), BF16),
        scratch_shapes=[pltpu.VMEM((RWKV_W // 128, HEAD_DIM, 128), F32),
                        pltpu.VMEM((tt, RWKV_W), F32)],
        compiler_params=_cparams(("arbitrary", "arbitrary")),
        name="rwkv7",
    )(rkv, lora, lora_w, vecs, ones, tri)


def _tail_kernel(x_ref, o0, o1, o2, l0, l1, l2, orw_ref, gate_ref, p_ref,
                 wau_ref, wru_ref, wout_ref, gpost_ref, gpre_ref, wfi_ref, wfo_ref, gffn_ref,
                 wple_ref, wpg_ref, gple_ref, out_ref):
    full = lambda ref: jnp.concatenate([ref[c] for c in range(ref.shape[0])], axis=1)
    ls = [full(l0), full(l1), full(l2)]
    m = jnp.maximum(jnp.maximum(ls[0], ls[1]), ls[2])
    ws = [jnp.exp(l - m) for l in ls]
    o_attn = (ws[0] * full(o0) + ws[1] * full(o1) + ws[2] * full(o2)) / (ws[0] + ws[1] + ws[2])
    d = x_ref.shape[1]
    ga = gate_ref[:, 0:d].astype(F32)
    gr = gate_ref[:, d:2 * d].astype(F32)
    merged = ga * _dot(o_attn.astype(BF16), wau_ref[...]) + gr * _dot(orw_ref[...], wru_ref[...])
    mo = _dot(merged.astype(BF16), wout_ref[...])
    h = x_ref[...] + _rms(mo, gpost_ref[...])

    f = _rms(h, gpre_ref[...]).astype(BF16)
    dff = wfi_ref.shape[1]
    nck = 4
    ck = dff // nck
    acc = jnp.zeros(h.shape, F32)
    for j in range(nck):
        a = jnp.maximum(_dot(f, wfi_ref[:, j * ck:(j + 1) * ck]), 0.0)
        acc = acc + _dot((a * a).astype(BF16), wfo_ref[j * ck:(j + 1) * ck, :])
    h = h + _rms(acc, gffn_ref[...])
    e = _dot(p_ref[...].astype(BF16), wple_ref[...]) * jax.nn.sigmoid(
        _dot(h.astype(BF16), wpg_ref[...]))
    out_ref[...] = h + _rms(e, gple_ref[...])


def _tail(x2, os_, ls_, orw, gates, p2, weights, tm):
    T, D = x2.shape
    row = lambda w: pl.BlockSpec((tm, w), lambda i: (i, 0))
    per_b = os_[0].shape[2] // tm
    attn = pl.BlockSpec((None, ATTN_W // 128, tm, 128), lambda i: (i // per_b, 0, i % per_b, 0))
    return pl.pallas_call(
        _tail_kernel,
        grid=(T // tm,),
        in_specs=[row(D)] + [attn] * 6 + [row(RWKV_W), row(2 * D), row(p2.shape[1])]
                 + [_const_spec(w.shape) for w in weights],
        out_specs=row(D),
        out_shape=jax.ShapeDtypeStruct((T, D), F32),
        compiler_params=_cparams(("arbitrary",)),
        name="tail",
    )(x2, *os_, *ls_, orw, gates, p2, *weights)


def _rope_tables(S):
    half = HEAD_DIM // 2
    inv_freq = ROPE_THETA ** (-jnp.arange(0, HEAD_DIM, 2, dtype=F32) / HEAD_DIM)
    ang = jnp.arange(S, dtype=F32)[:, None] * inv_freq[None, :]
    c, s = jnp.cos(ang), jnp.sin(ang)
    cos_t = jnp.tile(jnp.concatenate([c, c], axis=1), (1, 2))
    sin_t = jnp.tile(jnp.concatenate([-s, s], axis=1), (1, 2))
    scale = HEAD_DIM ** -0.5
    return jnp.concatenate([cos_t * scale, sin_t * scale, cos_t, sin_t], axis=1)


def _pad_slots(parts, axis):
    out = []
    for a in parts:
        pad = [(0, 0)] * a.ndim
        pad[axis] = (0, LORA_SLOT - a.shape[axis])
        out.append(jnp.pad(a, pad))
    return jnp.concatenate(out, axis=axis)


def kernel(x, p, mix_pre_norm, w_in, rwkv_mu, rwkv_w0, rwkv_w2, rwkv_a0, rwkv_a2, rwkv_g2,
           rwkv_k_k, rwkv_k_a, rwkv_r_k, rwkv_ln_w, rwkv_ln_b, w_attn_up, w_rwkv_up, w_out,
           mix_post_norm, ffn_pre_norm, w_ff_in, w_ff_out, ffn_post_norm, w_ple, w_ple_gate,
           ple_post_norm):
    B, S, D = x.shape
    depth = w_in.shape[0]
    nqk = 2 * N_GROUPS * ATTN_W
    nqkv = 3 * N_GROUPS * ATTN_W
    nr = 3 * RWKV_W
    lo = nqkv + nr
    h = x
    rope = _rope_tables(S)
    for i in range(depth):
        w = w_in[i]
        wl = _pad_slots([w[:, lo:lo + DECAY_LORA],
                         w[:, lo + DECAY_LORA:lo + DECAY_LORA + ICLR_LORA],
                         w[:, lo + DECAY_LORA + ICLR_LORA:lo + DECAY_LORA + ICLR_LORA + GATE_LORA]], 1)
        ng = lo + DECAY_LORA + ICLR_LORA + GATE_LORA
        mu = rwkv_mu[i]
        mul = _pad_slots([mu[nr:nr + DECAY_LORA], mu[nr + DECAY_LORA:nr + DECAY_LORA + ICLR_LORA],
                          mu[nr + DECAY_LORA + ICLR_LORA:]], 0)
        wqkv = w[:, :nqkv].reshape(D, 3, N_GROUPS, ATTN_W).transpose(0, 2, 1, 3).reshape(D, nqkv)
        q0, q1, q2, rkv, lora, gates = _in_proj(
            h, mix_pre_norm[i][None, :], wqkv.astype(BF16),
            w[:, nqkv:lo].astype(BF16), wl.astype(BF16), w[:, ng:].astype(BF16), rope,
            mu[None, :nr], mul[None, :], tm=min(512, S))

        os_, ls_ = [], []
        for qkv_g, (_, dil) in zip((q0, q1, q2), ATTN_GROUPS):
            o, lse = _attn_group(qkv_g, dil, tok=min(2048, S))
            os_.append(o)
            ls_.append(lse)

        lora_w = _pad_slots([rwkv_w2[i], rwkv_a2[i], rwkv_g2[i]], 0).astype(BF16)
        vecs = jnp.stack([rwkv_w0[i], rwkv_a0[i], rwkv_k_k[i], rwkv_k_a[i],
                          rwkv_r_k[i].reshape(-1), rwkv_ln_w[i], rwkv_ln_b[i],
                          jnp.zeros((RWKV_W,), F32)])
        orw = _rwkv(rkv, lora, lora_w, vecs, tt=min(1024, S))

        tm = min(512, S)
        weights = (w_attn_up[i].astype(BF16), w_rwkv_up[i].astype(BF16), w_out[i].astype(BF16),
                   mix_post_norm[i][None, :], ffn_pre_norm[i][None, :], w_ff_in[i].astype(BF16),
                   w_ff_out[i].astype(BF16), ffn_post_norm[i][None, :], w_ple[i].astype(BF16),
                   w_ple_gate[i].astype(BF16), ple_post_norm[i][None, :])
        h2 = _tail(h.reshape(B * S, D), os_, ls_, orw.reshape(B * S, RWKV_W),
                   gates.reshape(B * S, 2 * D), p[i].reshape(B * S, -1), weights, tm)
        h = h2.reshape(B, S, D)
    return h
```

```python
import functools

import numpy as np
import jax
import jax.numpy as jnp
from jax import lax
from jax.experimental import pallas as pl
from jax.experimental.pallas import tpu as pltpu

F32 = jnp.float32
BF16 = jnp.bfloat16

HEAD_DIM = 64
ATTN_GROUPS = ((128, 1), (512, 4), (2048, 16))
N_GROUPS = 3
ATTN_W = 256
ATTN_KEYS = 128
ROPE_THETA = 10000.0
RWKV_W = 512
RWKV_HEADS = 8
DECAY_LORA, ICLR_LORA, GATE_LORA = 32, 32, 96
LORA_SLOT = 128
RWKV_GN_EPS = 64e-5
NORM_EPS = 1e-6
CHUNK = 64
RWKV_SUB = 256
NEG = -1e30

VMEM_LIMIT = 56 * 1024 * 1024
MXU_DIM = 256


def _cparams(sem):
    return pltpu.CompilerParams(dimension_semantics=sem, vmem_limit_bytes=VMEM_LIMIT)


def _const_spec(shape):
    nd = len(shape)
    return pl.BlockSpec(shape, lambda *_: (0,) * nd, pipeline_mode=pl.Buffered(1))


def _dot(a, b):
    return jnp.dot(a, b, preferred_element_type=F32)


def _dot_nt(a, b):
    return lax.dot_general(a, b, (((1,), (1,)), ((), ())), preferred_element_type=F32)


def _dot_tn(a, b):
    return lax.dot_general(a, b, (((0,), (0,)), ((), ())), preferred_element_type=F32)


def _split3(x):
    hi = x.astype(BF16)
    r1 = x - hi.astype(F32)
    mid = r1.astype(BF16)
    lo = (r1 - mid.astype(F32)).astype(BF16)
    return hi, mid, lo


def _dot_exact_rhs(x, rhs_bf16, passes):
    parts = _split3(x)[:passes]
    acc = _dot(parts[0], rhs_bf16)
    for p in parts[1:]:
        acc = acc + _dot(p, rhs_bf16)
    return acc


def _exact_lhs_dot(lhs_bf16, x, passes):
    parts = _split3(x)[:passes]
    acc = _dot(lhs_bf16, parts[0])
    for p in parts[1:]:
        acc = acc + _dot(lhs_bf16, p)
    return acc


def _rms(x, g):
    ms = jnp.mean(x * x, axis=-1, keepdims=True)
    return x * lax.rsqrt(ms + NORM_EPS) * g


def _in_proj_kernel(x_ref, g_ref, wqkv_ref, wr_ref, wl_ref, wg_ref, rope_ref,
                    mur_ref, mul_ref, qkv0_ref, qkv1_ref, qkv2_ref, rkv_ref, lora_ref, gate_ref,
                    carry_r, carry_l, qkv_scr):
    s = pl.program_id(1)
    tm = x_ref.shape[0]

    @pl.when(s == 0)
    def _():
        carry_r[...] = jnp.zeros_like(carry_r)
        carry_l[...] = jnp.zeros_like(carry_l)

    u = _rms(x_ref[...], g_ref[...]).astype(BF16)
    gate_ref[...] = jax.nn.sigmoid(_dot(u, wg_ref[...])).astype(BF16)

    row0 = lax.broadcasted_iota(jnp.int32, (tm, 1), 0) == 0

    def shift_lerp(z_, carry, mu):
        zs = jnp.where(row0, carry[0:1, :], pltpu.roll(z_, 1, 0))
        carry[0:1, :] = z_[tm - 1:tm, :]
        return z_ + (zs - z_) * mu

    rkv_ref[...] = shift_lerp(_dot(u, wr_ref[...]), carry_r, mur_ref[...])

    z = _dot(u, wqkv_ref[...])
    lane = lax.broadcasted_iota(jnp.int32, (1, 128), 1)
    first_half = (lane % HEAD_DIM) < (HEAD_DIM // 2)
    gw = 3 * ATTN_W
    per_group = gw // 128
    halves = ATTN_W // 128
    for c in range(z.shape[1] // 128):
        typ, rest = divmod(c, N_GROUPS * halves)
        g, half = divmod(rest, halves)
        dst = typ * halves + half
        xc = z[:, c * 128:(c + 1) * 128]
        if typ < 2:
            t0 = 0 if typ == 0 else 256
            rot = jnp.where(first_half, pltpu.roll(xc, 96, 1), pltpu.roll(xc, 32, 1))
            xc = xc * rope_ref[:, t0:t0 + 128] + rot * rope_ref[:, t0 + 128:t0 + 256]
        if g == 0:
            qkv0_ref[0, :, dst * 128:(dst + 1) * 128] = xc.astype(BF16)
        else:
            qkv_scr[(g - 1) * per_group + dst] = xc
    for g, out_ref in ((1, qkv1_ref), (2, qkv2_ref)):
        dil = out_ref.shape[0]
        for r in range(dil):
            for cb in range(per_group):
                out_ref[r, :, cb * 128:(cb + 1) * 128] = qkv_scr[
                    (g - 1) * per_group + cb, pl.ds(r, tm // dil, stride=dil), :].astype(BF16)

    lora_ref[...] = shift_lerp(_dot(u, wl_ref[...]), carry_l, mul_ref[...])


def _in_proj(x, g, wqkv, wr, wl, wg, rope, mur, mul, tm):
    B, S, D = x.shape
    gw = 3 * ATTN_W
    row = lambda w: pl.BlockSpec((None, tm, w), lambda b, s: (b, s, 0))
    dils = [d for _, d in ATTN_GROUPS]
    qkv_specs = [pl.BlockSpec((None, d, tm // d, gw), lambda b, s: (b, 0, s, 0)) for d in dils]
    qkv_shapes = [jax.ShapeDtypeStruct((B, d, S // d, gw), BF16) for d in dils]
    return pl.pallas_call(
        _in_proj_kernel,
        grid=(B, S // tm),
        in_specs=[row(D), _const_spec(g.shape), _const_spec(wqkv.shape),
                  _const_spec(wr.shape), _const_spec(wl.shape), _const_spec(wg.shape),
                  pl.BlockSpec((tm, rope.shape[1]), lambda b, s: (s, 0)),
                  _const_spec(mur.shape), _const_spec(mul.shape)],
        out_specs=qkv_specs + [row(wr.shape[1]), row(wl.shape[1]), row(wg.shape[1])],
        out_shape=qkv_shapes + [jax.ShapeDtypeStruct((B, S, wr.shape[1]), F32),
                                jax.ShapeDtypeStruct((B, S, wl.shape[1]), F32),
                                jax.ShapeDtypeStruct((B, S, wg.shape[1]), BF16)],
        scratch_shapes=[pltpu.VMEM((8, wr.shape[1]), F32), pltpu.VMEM((8, wl.shape[1]), F32),
                        pltpu.VMEM(((N_GROUPS - 1) * gw // 128, tm, 128), F32)],
        compiler_params=_cparams(("arbitrary", "arbitrary")),
        name="in_proj",
    )(x, g, wqkv, wr, wl, wg, rope, mur, mul)


def _attn_kernel(cur_ref, kp_ref, vp_ref, o_ref, lse_ref, *, dil, nsub):
    i = pl.program_id(1)
    kw = 2 * ATTN_KEYS
    col = lax.broadcasted_iota(jnp.int32, (ATTN_KEYS, kw), 1)
    rowi = lax.broadcasted_iota(jnp.int32, (ATTN_KEYS, kw), 0)
    diff = col - rowi
    band = (diff >= 0) & (diff <= ATTN_KEYS)
    lane = lax.broadcasted_iota(jnp.int32, (1, ATTN_W), 1)
    hmask = [(lane // HEAD_DIM) == h for h in range(ATTN_W // HEAD_DIM)]
    zero_q = jnp.zeros((ATTN_KEYS, ATTN_W), BF16)
    low_half = lax.broadcasted_iota(jnp.int32, (1, 128), 1) < HEAD_DIM
    qs, ks, vs = slice(0, ATTN_W), slice(ATTN_W, 2 * ATTN_W), slice(2 * ATTN_W, 3 * ATTN_W)

    def load(u):
        r, sb = (u, 0) if nsub == 1 else ((0, u) if dil == 1 else (u // nsub, u % nsub))
        rows = pl.ds(pl.multiple_of(sb * ATTN_KEYS, ATTN_KEYS), ATTN_KEYS)
        q = cur_ref[r, rows, qs]
        if nsub == 1:
            k_lo, v_lo = kp_ref[r], vp_ref[r]
        else:
            lo = pl.ds(pl.multiple_of(jnp.maximum(sb - 1, 0) * ATTN_KEYS, ATTN_KEYS), ATTN_KEYS)
            first = sb == 0
            k_lo = jnp.where(first, kp_ref[r], cur_ref[r, lo, ks])
            v_lo = jnp.where(first, vp_ref[r], cur_ref[r, lo, vs])
        k = jnp.concatenate([k_lo, cur_ref[r, rows, ks]], axis=0)
        v = jnp.concatenate([v_lo, cur_ref[r, rows, vs]], axis=0)
        valid = band & ((col >= ATTN_KEYS) | (i > 0) | (sb > 0))
        return r, sb, q, k, v, valid

    def body(j, carry):
        us = [load(2 * j), load(2 * j + 1)]
        scs = [[jnp.where(valid, _dot_nt(jnp.where(hm, q, zero_q), k), NEG) for hm in hmask]
               for (_, _, q, k, _, valid) in us]
        ms = [[jnp.max(sc, axis=-1, keepdims=True) for sc in row] for row in scs]
        es = [[jnp.exp(sc - m) for sc, m in zip(r1, r2)] for r1, r2 in zip(scs, ms)]
        dens = [[jnp.sum(e, axis=-1, keepdims=True) for e in row] for row in es]
        ohs = [[_dot(e.astype(BF16), u_[4]) for e in row] for row, u_ in zip(es, us)]
        for (r, sb, *_), oh_r, den_r, m_r in zip(us, ohs, dens, ms):
            if dil == 1:
                dst = pl.ds(pl.multiple_of(sb * ATTN_KEYS, ATTN_KEYS), ATTN_KEYS)
            else:
                dst = pl.ds(r + sb * (ATTN_KEYS * dil), ATTN_KEYS, stride=dil)
            for cb in range(ATTN_W // 128):
                cols = slice(cb * 128, (cb + 1) * 128)
                h0, h1 = 2 * cb, 2 * cb + 1
                o_ref[cb, dst, :] = jnp.where(low_half, oh_r[h0][:, cols] / den_r[h0],
                                              oh_r[h1][:, cols] / den_r[h1])
                lse_ref[cb, dst, :] = jnp.where(low_half, m_r[h0] + jnp.log(den_r[h0]),
                                                m_r[h1] + jnp.log(den_r[h1]))
        return carry

    lax.fori_loop(0, dil * nsub // 2, body, 0)


def _attn_group(qkv_g, dil, tok):
    B, _, n, gw = qkv_g.shape
    S = n * dil
    tq = tok // dil
    nsub = tq // ATTN_KEYS
    cur = pl.BlockSpec((None, dil, tq, gw), lambda b, i: (b, 0, i, 0))
    prev = lambda c: pl.BlockSpec((None, dil, ATTN_KEYS, ATTN_W),
                                  lambda b, i: (b, 0, jnp.maximum(i * nsub - 1, 0), c))
    out = pl.BlockSpec((None, ATTN_W // 128, tok, 128), lambda b, i: (b, 0, i, 0))
    return pl.pallas_call(
        functools.partial(_attn_kernel, dil=dil, nsub=nsub),
        grid=(B, S // tok),
        in_specs=[cur, prev(1), prev(2)],
        out_specs=[out, out],
        out_shape=[jax.ShapeDtypeStruct((B, ATTN_W // 128, S, 128), F32)] * 2,
        compiler_params=_cparams(("arbitrary", "arbitrary")),
        name=f"attn_d{dil}",
    )(qkv_g, qkv_g, qkv_g)


def _bd(x):
    lane = lax.broadcasted_iota(jnp.int32, (1, x.shape[1]), 1)
    m0 = (lane % 128) < HEAD_DIM
    z = jnp.zeros_like(x)
    return jnp.concatenate([jnp.where(m0, x, z), jnp.where(m0, z, x)], axis=0)


def _rwkv_kernel(rkv_ref, lora_ref, lw_ref, vec_ref, ones_ref, tri_ref,
                 o_ref, st_ref, y_scr):
    s = pl.program_id(1)
    nsub = rkv_ref.shape[0] // RWKV_SUB

    @pl.when(s == 0)
    def _():
        st_ref[...] = jnp.zeros_like(st_ref)

    st_box = [[st_ref[p] for p in range(RWKV_W // 128)]]
    subs = [_rwkv_sub(i * RWKV_SUB, rkv_ref, lora_ref, lw_ref, vec_ref, ones_ref, tri_ref, o_ref,
                      y_scr, st_box) for i in range(nsub)]

    def advance_to(gen, tag):
        while next(gen) != tag:
            pass

    advance_to(subs[0], "prologue done")
    for i in range(nsub):
        main_done, prev_done, next_done = False, i == 0, i + 1 == nsub
        while not (main_done and prev_done and next_done):
            if not main_done:
                main_done = next(subs[i]) == "main done"
            if not prev_done:
                prev_done = next(subs[i - 1], None) is None
            if not next_done:
                next_done = next(subs[i + 1]) == "prologue done"
    for _ in subs[nsub - 1]:
        pass
    for p, st in enumerate(st_box[0]):
        st_ref[p] = st


def _rwkv_sub(base, rkv_ref, lora_ref, lw_ref, vec_ref, ones_ref, tri_ref, o_ref, y_scr, st_box):
    tt = RWKV_SUB
    nchunk = tt // CHUNK
    npair = RWKV_W // 128
    span = slice(base, base + tt)

    w0, a0, k_k, k_a, r_k, ln_w, ln_b = [vec_ref[j:j + 1, :] for j in range(7)]
    r = rkv_ref[span, 0:RWKV_W]
    k = rkv_ref[span, RWKV_W:2 * RWKV_W]
    v = rkv_ref[span, 2 * RWKV_W:3 * RWKV_W]
    xw = lora_ref[span, 0:LORA_SLOT]
    xa = lora_ref[span, LORA_SLOT:2 * LORA_SLOT]
    xg = lora_ref[span, 2 * LORA_SLOT:3 * LORA_SLOT]
    ones = ones_ref[...]
    ow = ones.shape[0]

    def head_sum(x, passes):
        return jnp.concatenate([_dot_exact_rhs(x[:, c:c + ow], ones, passes)
                                for c in range(0, RWKV_W, ow)], axis=1)

    def lora_dot(x, j):
        return _dot(x.astype(BF16), lw_ref[j * LORA_SLOT:(j + 1) * LORA_SLOT, :])

    y_w = -(w0 + lora_dot(jnp.tanh(xw), 0))
    softplus = jnp.maximum(y_w, 0.0) + jnp.log(1.0 + jnp.exp(-jnp.abs(y_w)))
    ld = -jnp.exp(-softplus - 0.5)
    yield "prologue"
    cum = _exact_lhs_dot(tri_ref[...], ld, 2)
    yield "prologue"
    a = jax.nn.sigmoid(a0 + lora_dot(xa, 1))
    g = lora_dot(jax.nn.sigmoid(xg), 2)
    yield "prologue"
    kk = k * k_k
    kk = kk / jnp.maximum(jnp.sqrt(head_sum(kk * kk, 1)), 1e-12)
    yield "prologue"
    kp = k * (1.0 + (a - 1.0) * k_a)
    avec = -kk
    bvec = kk * a
    yield "prologue"

    lane = lax.broadcasted_iota(jnp.int32, (1, 128), 1)
    m0 = lane < HEAD_DIM
    ri = lax.broadcasted_iota(jnp.int32, (CHUNK, 128), 0)
    ci = lax.broadcasted_iota(jnp.int32, (CHUNK, 128), 1) % HEAD_DIM
    strict = ci < ri
    incl = ci <= ri
    eye = ci == ri
    blk16 = (ci // 16) == (ri // 16)
    blk32 = (ci // 32) == (ri // 32)

    units = [(slice(c * CHUNK, (c + 1) * CHUNK), slice(p * 128, (p + 1) * 128))
             for c in range(nchunk) for p in range(npair)]
    each = lambda fn, *lists: [fn(*args) for args in zip(*lists)]
    cat0 = lambda *xs: jnp.concatenate(xs, axis=0)
    cat1 = lambda *xs: jnp.concatenate(xs, axis=1)

    def prep(u):
        rows, cols = u
        cm = cum[rows, cols]
        c_last = cm[CHUNK - 1:CHUNK, :]
        w_in = jnp.exp(cm)
        w_ex = jnp.exp(cm - ld[rows, cols])
        w_inv = jnp.exp(-cm)
        w_out = jnp.exp(c_last - cm)
        r_f = r[rows, cols] * w_in
        return dict(
            r_f=r_f, r_t=r_f.astype(BF16),
            a_t=(avec[rows, cols] * w_ex).astype(BF16),
            b_t=(bvec[rows, cols] * w_inv).astype(BF16),
            k_t=(kp[rows, cols] * w_inv).astype(BF16),
            b_o=(bvec[rows, cols] * w_out).astype(BF16),
            k_o=(kp[rows, cols] * w_out).astype(BF16),
            vv=v[rows, cols].astype(BF16),
            wc=jnp.where(eye, jnp.exp(c_last), 0.0))

    P = []
    for c in range(nchunk):
        P += each(prep, units[c * npair:(c + 1) * npair])
        yield "prologue"
    yield "prologue done"

    sc = each(lambda d: _dot_nt(cat0(d["a_t"], d["r_t"]), cat0(_bd(d["b_t"]), _bd(d["k_t"]))), P)
    yield "main"
    l_ab = [jnp.where(strict, x[0:CHUNK, 0:128], 0.0) for x in sc]
    m_ak = [jnp.where(strict, x[0:CHUNK, 128:256], 0.0) for x in sc]
    m_rb = [jnp.where(incl, x[CHUNK:, 0:128], 0.0).astype(BF16) for x in sc]
    m_rk = [jnp.where(incl, x[CHUNK:, 128:256], 0.0) for x in sc]

    mv = each(lambda ak, rk, d: _dot(cat0(ak, rk).astype(BF16), _bd(d["vv"])), m_ak, m_rk, P)
    yield "main"
    makv = [x[0:CHUNK].astype(BF16) for x in mv]
    mrkv = [x[CHUNK:] for x in mv]

    pw = [jnp.where(blk16, x, 0.0).astype(BF16) for x in l_ab]
    tmat = [jnp.where(eye, 1.0, 0.0) for _ in units]
    for it in range(4):
        if it < 3:
            z = each(lambda a, t: _dot(a, _bd(cat1(a, t.astype(BF16)))), pw, tmat)
            pw = [x[:, 0:128].astype(BF16) for x in z]
            tmat = each(lambda t, x: t + x[:, 128:256], tmat, z)
        else:
            tmat = each(lambda a, t: t + _dot(a, _bd(t.astype(BF16))), pw, tmat)
        yield "main"
    for sel in (lambda x: jnp.where(blk32 & ~blk16, x, 0.0), lambda x: jnp.where(blk32, 0.0, x)):
        off = [sel(x).astype(BF16) for x in l_ab]
        t16 = [t.astype(BF16) for t in tmat]
        ot = each(lambda o, t: _dot(o, _bd(t)).astype(BF16), off, t16)
        yield "main"
        tmat = each(lambda t, t6, o: t + _dot(t6, _bd(o)), tmat, t16, ot)
        yield "main"

    xh = each(lambda t, d, m: _dot(t.astype(BF16), _bd(cat1(d["a_t"], m))), tmat, P, makv)
    yield "main"
    ahv = [x.astype(BF16) for x in xh]

    mr = each(lambda m, x: _dot(m, _bd(x)), m_rb, ahv)
    yield "main"
    r_hat = each(lambda d, x: d["r_f"] + x[:, 0:128], P, mr)
    y_intra = each(lambda x, y: x[:, 128:256] + y, mr, mrkv)

    gh = each(lambda d, x: _dot_tn(cat0(d["b_o"], d["k_o"]),
                                   cat0(x, cat1(jnp.zeros_like(d["vv"]), d["vv"]))), P, ahv)
    gt = each(lambda x, d: jnp.where(m0, x[0:CHUNK, 0:128], x[CHUNK:, 0:128]) + d["wc"], gh, P)
    ht = [jnp.where(m0, x[0:CHUNK, 128:256], x[CHUNK:, 128:256]) for x in gh]
    lhs = each(lambda a, b: cat0(a, b).astype(BF16), r_hat, gt)
    yield "main done"

    st = st_box[0]
    for c in range(nchunk):
        sl = slice(c * npair, (c + 1) * npair)
        ys = each(lambda a, s_: _dot(a, _bd(s_.astype(BF16))), lhs[sl], st)
        for (rows, cols), y_, yi in zip(units[sl], ys, y_intra[sl]):
            y_scr[base + rows.start:base + rows.stop, cols] = y_[0:CHUNK] + yi
        st = each(lambda y_, h_: y_[CHUNK:] + h_, ys, ht[sl])
        yield "state pass"
    st_box[0] = st

    y = y_scr[span, :]
    inv_n = 1.0 / HEAD_DIM
    mean = head_sum(y, 2) * inv_n
    yield "epilogue"
    yc = y - mean
    var = head_sum(yc * yc, 1) * inv_n
    yield "epilogue"
    yn = yc * lax.rsqrt(var + RWKV_GN_EPS) * ln_w + ln_b
    bonus = head_sum(r * kp * r_k, 1) * v
    o_ref[span, :] = ((yn + bonus) * g).astype(o_ref.dtype)


def _rwkv(rkv, lora, lora_w, vecs, tt):
    B, S, _ = rkv.shape
    ones = jnp.asarray(np.kron(np.eye(MXU_DIM // HEAD_DIM), np.ones((HEAD_DIM, HEAD_DIM))), BF16)
    tri = jnp.asarray(np.kron(np.eye(RWKV_SUB // CHUNK), np.tril(np.ones((CHUNK, CHUNK)))), BF16)
    row = lambda w: pl.BlockSpec((None, tt, w), lambda b, s: (b, s, 0))
    return pl.pallas_call(
        _rwkv_kernel,
        grid=(B, S // tt),
        in_specs=[row(rkv.shape[2]), row(lora.shape[2]), _const_spec(lora_w.shape),
                  _const_spec(vecs.shape), _const_spec(ones.shape), _const_spec(tri.shape)],
        out_specs=row(RWKV_W),
        out_shape=jax.ShapeDtypeStruct((B, S, RWKV_W), BF16),
        scratch_shapes=[pltpu.VMEM((RWKV_W // 128, HEAD_DIM, 128), F32),
                        pltpu.VMEM((tt, RWKV_W), F32)],
        compiler_params=_cparams(("arbitrary", "arbitrary")),
        name="rwkv7",
    )(rkv, lora, lora_w, vecs, ones, tri)


def _tail_kernel(x_ref, o0, o1, o2, l0, l1, l2, orw_ref, gate_ref, p_ref,
                 wau_ref, wru_ref, wout_ref, gpost_ref, gpre_ref, wfi_ref, wfo_ref, gffn_ref,
                 wple_ref, wpg_ref, gple_ref, out_ref):
    full = lambda ref: jnp.concatenate([ref[c] for c in range(ref.shape[0])], axis=1)
    ls = [full(l0), full(l1), full(l2)]
    m = jnp.maximum(jnp.maximum(ls[0], ls[1]), ls[2])
    ws = [jnp.exp(l - m) for l in ls]
    o_attn = (ws[0] * full(o0) + ws[1] * full(o1) + ws[2] * full(o2)) / (ws[0] + ws[1] + ws[2])
    d = x_ref.shape[1]
    ga = gate_ref[:, 0:d].astype(F32)
    gr = gate_ref[:, d:2 * d].astype(F32)
    merged = ga * _dot(o_attn.astype(BF16), wau_ref[...]) + gr * _dot(orw_ref[...], wru_ref[...])
    mo = _dot(merged.astype(BF16), wout_ref[...])
    h = x_ref[...] + _rms(mo, gpost_ref[...])

    f = _rms(h, gpre_ref[...]).astype(BF16)
    dff = wfi_ref.shape[1]
    nck = 4
    ck = dff // nck
    acc = jnp.zeros(h.shape, F32)
    for j in range(nck):
        a = jnp.maximum(_dot(f, wfi_ref[:, j * ck:(j + 1) * ck]), 0.0)
        acc = acc + _dot((a * a).astype(BF16), wfo_ref[j * ck:(j + 1) * ck, :])
    h = h + _rms(acc, gffn_ref[...])
    e = _dot(p_ref[...].astype(BF16), wple_ref[...]) * jax.nn.sigmoid(
        _dot(h.astype(BF16), wpg_ref[...]))
    out_ref[...] = h + _rms(e, gple_ref[...])


def _tail(x2, os_, ls_, orw, gates, p2, weights, tm):
    T, D = x2.shape
    row = lambda w: pl.BlockSpec((tm, w), lambda i: (i, 0))
    per_b = os_[0].shape[2] // tm
    attn = pl.BlockSpec((None, ATTN_W // 128, tm, 128), lambda i: (i // per_b, 0, i % per_b, 0))
    return pl.pallas_call(
        _tail_kernel,
        grid=(T // tm,),
        in_specs=[row(D)] + [attn] * 6 + [row(RWKV_W), row(2 * D), row(p2.shape[1])]
                 + [_const_spec(w.shape) for w in weights],
        out_specs=row(D),
        out_shape=jax.ShapeDtypeStruct((T, D), F32),
        compiler_params=_cparams(("arbitrary",)),
        name="tail",
    )(x2, *os_, *ls_, orw, gates, p2, *weights)


def _rope_tables(S):
    half = HEAD_DIM // 2
    inv_freq = ROPE_THETA ** (-jnp.arange(0, HEAD_DIM, 2, dtype=F32) / HEAD_DIM)
    ang = jnp.arange(S, dtype=F32)[:, None] * inv_freq[None, :]
    c, s = jnp.cos(ang), jnp.sin(ang)
    cos_t = jnp.tile(jnp.concatenate([c, c], axis=1), (1, 2))
    sin_t = jnp.tile(jnp.concatenate([-s, s], axis=1), (1, 2))
    scale = HEAD_DIM ** -0.5
    return jnp.concatenate([cos_t * scale, sin_t * scale, cos_t, sin_t], axis=1)


def _pad_slots(parts, axis):
    out = []
    for a in parts:
        pad = [(0, 0)] * a.ndim
        pad[axis] = (0, LORA_SLOT - a.shape[axis])
        out.append(jnp.pad(a, pad))
    return jnp.concatenate(out, axis=axis)


def kernel(x, p, mix_pre_norm, w_in, rwkv_mu, rwkv_w0, rwkv_w2, rwkv_a0, rwkv_a2, rwkv_g2,
           rwkv_k_k, rwkv_k_a, rwkv_r_k, rwkv_ln_w, rwkv_ln_b, w_attn_up, w_rwkv_up, w_out,
           mix_post_norm, ffn_pre_norm, w_ff_in, w_ff_out, ffn_post_norm, w_ple, w_ple_gate,
           ple_post_norm):
    B, S, D = x.shape
    depth = w_in.shape[0]
    nqkv = 3 * N_GROUPS * ATTN_W
    nr = 3 * RWKV_W
    lo = nqkv + nr
    h = x
    rope = _rope_tables(S)
    for i in range(depth):
        w = w_in[i]
        wl = _pad_slots([w[:, lo:lo + DECAY_LORA],
                         w[:, lo + DECAY_LORA:lo + DECAY_LORA + ICLR_LORA],
                         w[:, lo + DECAY_LORA + ICLR_LORA:lo + DECAY_LORA + ICLR_LORA + GATE_LORA]], 1)
        ng = lo + DECAY_LORA + ICLR_LORA + GATE_LORA
        mu = rwkv_mu[i]
        mul = _pad_slots([mu[nr:nr + DECAY_LORA], mu[nr + DECAY_LORA:nr + DECAY_LORA + ICLR_LORA],
                          mu[nr + DECAY_LORA + ICLR_LORA:]], 0)
        q0, q1, q2, rkv, lora, gates = _in_proj(
            h, mix_pre_norm[i][None, :], w[:, :nqkv].astype(BF16),
            w[:, nqkv:lo].astype(BF16), wl.astype(BF16), w[:, ng:].astype(BF16), rope,
            mu[None, :nr], mul[None, :], tm=min(512, S))

        os_, ls_ = [], []
        for qkv_g, (_, dil) in zip((q0, q1, q2), ATTN_GROUPS):
            o, lse = _attn_group(qkv_g, dil, tok=min(2048, S))
            os_.append(o)
            ls_.append(lse)

        lora_w = _pad_slots([rwkv_w2[i], rwkv_a2[i], rwkv_g2[i]], 0).astype(BF16)
        vecs = jnp.stack([rwkv_w0[i], rwkv_a0[i], rwkv_k_k[i], rwkv_k_a[i],
                          rwkv_r_k[i].reshape(-1), rwkv_ln_w[i], rwkv_ln_b[i],
                          jnp.zeros((RWKV_W,), F32)])
        orw = _rwkv(rkv, lora, lora_w, vecs, tt=min(1024, S))

        tm = min(512, S)
        weights = (w_attn_up[i].astype(BF16), w_rwkv_up[i].astype(BF16), w_out[i].astype(BF16),
                   mix_post_norm[i][None, :], ffn_pre_norm[i][None, :], w_ff_in[i].astype(BF16),
                   w_ff_out[i].astype(BF16), ffn_post_norm[i][None, :], w_ple[i].astype(BF16),
                   w_ple_gate[i].astype(BF16), ple_post_norm[i][None, :])
        h2 = _tail(h.reshape(B * S, D), os_, ls_, orw.reshape(B * S, RWKV_W),
                   gates.reshape(B * S, 2 * D), p[i].reshape(B * S, -1), weights, tm)
        h = h2.reshape(B, S, D)
    return h
```

```python
import functools

import numpy as np
import jax
import jax.numpy as jnp
from jax import lax
from jax.experimental import pallas as pl
from jax.experimental.pallas import tpu as pltpu

F32 = jnp.float32
BF16 = jnp.bfloat16

HEAD_DIM = 64
ATTN_GROUPS = ((128, 1), (512, 4), (2048, 16))
N_GROUPS = 3
ATTN_W = 256
ATTN_KEYS = 128
ROPE_THETA = 10000.0
RWKV_W = 512
RWKV_HEADS = 8
DECAY_LORA, ICLR_LORA, GATE_LORA = 32, 32, 96
LORA_SLOT = 128
RWKV_GN_EPS = 64e-5
NORM_EPS = 1e-6
CHUNK = 64
RWKV_SUB = 256
NEG = -1e30

VMEM_LIMIT = 56 * 1024 * 1024
MXU_DIM = 256


def _cparams(sem):
    return pltpu.CompilerParams(dimension_semantics=sem, vmem_limit_bytes=VMEM_LIMIT)


def _const_spec(shape):
    nd = len(shape)
    return pl.BlockSpec(shape, lambda *_: (0,) * nd, pipeline_mode=pl.Buffered(1))


def _dot(a, b):
    return jnp.dot(a, b, preferred_element_type=F32)


def _dot_nt(a, b):
    return lax.dot_general(a, b, (((1,), (1,)), ((), ())), preferred_element_type=F32)


def _dot_tn(a, b):
    return lax.dot_general(a, b, (((0,), (0,)), ((), ())), preferred_element_type=F32)


def _split3(x):
    hi = x.astype(BF16)
    r1 = x - hi.astype(F32)
    mid = r1.astype(BF16)
    lo = (r1 - mid.astype(F32)).astype(BF16)
    return hi, mid, lo


def _dot_exact_rhs(x, rhs_bf16, passes):
    parts = _split3(x)[:passes]
    acc = _dot(parts[0], rhs_bf16)
    for p in parts[1:]:
        acc = acc + _dot(p, rhs_bf16)
    return acc


def _exact_lhs_dot(lhs_bf16, x, passes):
    parts = _split3(x)[:passes]
    acc = _dot(lhs_bf16, parts[0])
    for p in parts[1:]:
        acc = acc + _dot(lhs_bf16, p)
    return acc


def _rms(x, g):
    ms = jnp.mean(x * x, axis=-1, keepdims=True)
    return x * lax.rsqrt(ms + NORM_EPS) * g


def _in_proj_kernel(x_ref, g_ref, wqkv_ref, wr_ref, wl_ref, wg_ref, rope_ref,
                    mur_ref, mul_ref, qkv0_ref, qkv1_ref, qkv2_ref, rkv_ref, lora_ref, gate_ref,
                    carry_r, carry_l, qkv_scr):
    s = pl.program_id(1)
    tm = x_ref.shape[0]

    @pl.when(s == 0)
    def _():
        carry_r[...] = jnp.zeros_like(carry_r)
        carry_l[...] = jnp.zeros_like(carry_l)

    u = _rms(x_ref[...], g_ref[...]).astype(BF16)
    gate_ref[...] = jax.nn.sigmoid(_dot(u, wg_ref[...])).astype(BF16)

    row0 = lax.broadcasted_iota(jnp.int32, (tm, 1), 0) == 0

    def shift_lerp(z_, carry, mu):
        zs = jnp.where(row0, carry[0:1, :], pltpu.roll(z_, 1, 0))
        carry[0:1, :] = z_[tm - 1:tm, :]
        return z_ + (zs - z_) * mu

    rkv_ref[...] = shift_lerp(_dot(u, wr_ref[...]), carry_r, mur_ref[...])

    z = _dot(u, wqkv_ref[...])
    lane = lax.broadcasted_iota(jnp.int32, (1, 128), 1)
    first_half = (lane % HEAD_DIM) < (HEAD_DIM // 2)
    gw = 3 * ATTN_W
    per_group = gw // 128
    halves = ATTN_W // 128
    for c in range(z.shape[1] // 128):
        typ, rest = divmod(c, N_GROUPS * halves)
        g, half = divmod(rest, halves)
        dst = typ * halves + half
        xc = z[:, c * 128:(c + 1) * 128]
        if typ < 2:
            t0 = 0 if typ == 0 else 256
            rot = jnp.where(first_half, pltpu.roll(xc, 96, 1), pltpu.roll(xc, 32, 1))
            xc = xc * rope_ref[:, t0:t0 + 128] + rot * rope_ref[:, t0 + 128:t0 + 256]
        if g == 0:
            qkv0_ref[0, :, dst * 128:(dst + 1) * 128] = xc.astype(BF16)
        else:
            qkv_scr[(g - 1) * per_group + dst] = xc
    for g, out_ref in ((1, qkv1_ref), (2, qkv2_ref)):
        dil = out_ref.shape[0]
        for r in range(dil):
            for cb in range(per_group):
                out_ref[r, :, cb * 128:(cb + 1) * 128] = qkv_scr[
                    (g - 1) * per_group + cb, pl.ds(r, tm // dil, stride=dil), :].astype(BF16)

    lora_ref[...] = shift_lerp(_dot(u, wl_ref[...]), carry_l, mul_ref[...])


def _in_proj(x, g, wqkv, wr, wl, wg, rope, mur, mul, tm):
    B, S, D = x.shape
    gw = 3 * ATTN_W
    row = lambda w: pl.BlockSpec((None, tm, w), lambda b, s: (b, s, 0))
    dils = [d for _, d in ATTN_GROUPS]
    qkv_specs = [pl.BlockSpec((None, d, tm // d, gw), lambda b, s: (b, 0, s, 0)) for d in dils]
    qkv_shapes = [jax.ShapeDtypeStruct((B, d, S // d, gw), BF16) for d in dils]
    return pl.pallas_call(
        _in_proj_kernel,
        grid=(B, S // tm),
        in_specs=[row(D), _const_spec(g.shape), _const_spec(wqkv.shape),
                  _const_spec(wr.shape), _const_spec(wl.shape), _const_spec(wg.shape),
                  pl.BlockSpec((tm, rope.shape[1]), lambda b, s: (s, 0)),
                  _const_spec(mur.shape), _const_spec(mul.shape)],
        out_specs=qkv_specs + [row(wr.shape[1]), row(wl.shape[1]), row(wg.shape[1])],
        out_shape=qkv_shapes + [jax.ShapeDtypeStruct((B, S, wr.shape[1]), F32),
                                jax.ShapeDtypeStruct((B, S, wl.shape[1]), F32),
                                jax.ShapeDtypeStruct((B, S, wg.shape[1]), BF16)],
        scratch_shapes=[pltpu.VMEM((8, wr.shape[1]), F32), pltpu.VMEM((8, wl.shape[1]), F32),
                        pltpu.VMEM(((N_GROUPS - 1) * gw // 128, tm, 128), F32)],
        compiler_params=_cparams(("arbitrary", "arbitrary")),
        name="in_proj",
    )(x, g, wqkv, wr, wl, wg, rope, mur, mul)


def _attn_kernel(cur_ref, kp_ref, vp_ref, o_ref, lse_ref, *, dil, nsub):
    i = pl.program_id(1)
    kw = 2 * ATTN_KEYS
    col = lax.broadcasted_iota(jnp.int32, (ATTN_KEYS, kw), 1)
    rowi = lax.broadcasted_iota(jnp.int32, (ATTN_KEYS, kw), 0)
    diff = col - rowi
    band = (diff >= 0) & (diff <= ATTN_KEYS)
    lane = lax.broadcasted_iota(jnp.int32, (1, ATTN_W), 1)
    hmask = [(lane // HEAD_DIM) == h for h in range(ATTN_W // HEAD_DIM)]
    zero_q = jnp.zeros((ATTN_KEYS, ATTN_W), BF16)
    low_half = lax.broadcasted_iota(jnp.int32, (1, 128), 1) < HEAD_DIM
    qs, ks, vs = slice(0, ATTN_W), slice(ATTN_W, 2 * ATTN_W), slice(2 * ATTN_W, 3 * ATTN_W)

    def load(u):
        r, sb = (u, 0) if nsub == 1 else ((0, u) if dil == 1 else (u // nsub, u % nsub))
        rows = pl.ds(pl.multiple_of(sb * ATTN_KEYS, ATTN_KEYS), ATTN_KEYS)
        q = cur_ref[r, rows, qs]
        if nsub == 1:
            k_lo, v_lo = kp_ref[r], vp_ref[r]
        else:
            lo = pl.ds(pl.multiple_of(jnp.maximum(sb - 1, 0) * ATTN_KEYS, ATTN_KEYS), ATTN_KEYS)
            first = sb == 0
            k_lo = jnp.where(first, kp_ref[r], cur_ref[r, lo, ks])
            v_lo = jnp.where(first, vp_ref[r], cur_ref[r, lo, vs])
        k = jnp.concatenate([k_lo, cur_ref[r, rows, ks]], axis=0)
        v = jnp.concatenate([v_lo, cur_ref[r, rows, vs]], axis=0)
        valid = band & ((col >= ATTN_KEYS) | (i > 0) | (sb > 0))
        return r, sb, q, k, v, valid

    def body(j, carry):
        us = [load(2 * j), load(2 * j + 1)]
        scs = [[jnp.where(valid, _dot_nt(jnp.where(hm, q, zero_q), k), NEG) for hm in hmask]
               for (_, _, q, k, _, valid) in us]
        ms = [[jnp.max(sc, axis=-1, keepdims=True) for sc in row] for row in scs]
        es = [[jnp.exp(sc - m) for sc, m in zip(r1, r2)] for r1, r2 in zip(scs, ms)]
        dens = [[jnp.sum(e, axis=-1, keepdims=True) for e in row] for row in es]
        ohs = [[_dot(e.astype(BF16), u_[4]) for e in row] for row, u_ in zip(es, us)]
        for (r, sb, *_), oh_r, den_r, m_r in zip(us, ohs, dens, ms):
            if dil == 1:
                dst = pl.ds(pl.multiple_of(sb * ATTN_KEYS, ATTN_KEYS), ATTN_KEYS)
            else:
                dst = pl.ds(r + sb * (ATTN_KEYS * dil), ATTN_KEYS, stride=dil)
            for cb in range(ATTN_W // 128):
                cols = slice(cb * 128, (cb + 1) * 128)
                h0, h1 = 2 * cb, 2 * cb + 1
                o_ref[cb, dst, :] = jnp.where(low_half, oh_r[h0][:, cols] / den_r[h0],
                                              oh_r[h1][:, cols] / den_r[h1])
                lse_ref[cb, dst, :] = jnp.where(low_half, m_r[h0] + jnp.log(den_r[h0]),
                                                m_r[h1] + jnp.log(den_r[h1]))
        return carry

    lax.fori_loop(0, dil * nsub // 2, body, 0)


def _attn_group(qkv_g, dil, tok):
    B, _, n, gw = qkv_g.shape
    S = n * dil
    tq = tok // dil
    nsub = tq // ATTN_KEYS
    cur = pl.BlockSpec((None, dil, tq, gw), lambda b, i: (b, 0, i, 0))
    prev = lambda c: pl.BlockSpec((None, dil, ATTN_KEYS, ATTN_W),
                                  lambda b, i: (b, 0, jnp.maximum(i * nsub - 1, 0), c))
    out = pl.BlockSpec((None, ATTN_W // 128, tok, 128), lambda b, i: (b, 0, i, 0))
    return pl.pallas_call(
        functools.partial(_attn_kernel, dil=dil, nsub=nsub),
        grid=(B, S // tok),
        in_specs=[cur, prev(1), prev(2)],
        out_specs=[out, out],
        out_shape=[jax.ShapeDtypeStruct((B, ATTN_W // 128, S, 128), F32)] * 2,
        compiler_params=_cparams(("arbitrary", "arbitrary")),
        name=f"attn_d{dil}",
    )(qkv_g, qkv_g, qkv_g)


def _bd(x):
    lane = lax.broadcasted_iota(jnp.int32, (1, x.shape[1]), 1)
    m0 = (lane % 128) < HEAD_DIM
    z = jnp.zeros_like(x)
    return jnp.concatenate([jnp.where(m0, x, z), jnp.where(m0, z, x)], axis=0)


def _rwkv_kernel(rkv_ref, lora_ref, lw_ref, vec_ref, ones_ref, tri_ref,
                 o_ref, st_ref, y_scr):
    s = pl.program_id(1)
    nsub = rkv_ref.shape[0] // RWKV_SUB

    @pl.when(s == 0)
    def _():
        st_ref[...] = jnp.zeros_like(st_ref)

    st_box = [[st_ref[p] for p in range(RWKV_W // 128)]]
    subs = [_rwkv_sub(i * RWKV_SUB, rkv_ref, lora_ref, lw_ref, vec_ref, ones_ref, tri_ref, o_ref,
                      y_scr, st_box) for i in range(nsub)]

    def advance_to(gen, tag):
        while next(gen) != tag:
            pass

    advance_to(subs[0], "prologue done")
    for i in range(nsub):
        main_done, prev_done, next_done = False, i == 0, i + 1 == nsub
        while not (main_done and prev_done and next_done):
            if not main_done:
                main_done = next(subs[i]) == "main done"
            if not prev_done:
                prev_done = next(subs[i - 1], None) is None
            if not next_done:
                next_done = next(subs[i + 1]) == "prologue done"
    for _ in subs[nsub - 1]:
        pass
    for p, st in enumerate(st_box[0]):
        st_ref[p] = st


def _rwkv_sub(base, rkv_ref, lora_ref, lw_ref, vec_ref, ones_ref, tri_ref, o_ref, y_scr, st_box):
    tt = RWKV_SUB
    nchunk = tt // CHUNK
    npair = RWKV_W // 128
    span = slice(base, base + tt)

    w0, a0, k_k, k_a, r_k, ln_w, ln_b = [vec_ref[j:j + 1, :] for j in range(7)]
    r = rkv_ref[span, 0:RWKV_W]
    k = rkv_ref[span, RWKV_W:2 * RWKV_W]
    v = rkv_ref[span, 2 * RWKV_W:3 * RWKV_W]
    xw = lora_ref[span, 0:LORA_SLOT]
    xa = lora_ref[span, LORA_SLOT:2 * LORA_SLOT]
    xg = lora_ref[span, 2 * LORA_SLOT:3 * LORA_SLOT]
    ones = ones_ref[...]
    ow = ones.shape[0]

    def head_sum(x, passes):
        return jnp.concatenate([_dot_exact_rhs(x[:, c:c + ow], ones, passes)
                                for c in range(0, RWKV_W, ow)], axis=1)

    def lora_dot(x, j):
        return _dot(x.astype(BF16), lw_ref[j * LORA_SLOT:(j + 1) * LORA_SLOT, :])

    y_w = -(w0 + lora_dot(jnp.tanh(xw), 0))
    softplus = jnp.maximum(y_w, 0.0) + jnp.log(1.0 + jnp.exp(-jnp.abs(y_w)))
    ld = -jnp.exp(-softplus - 0.5)
    yield "prologue"
    cum = _exact_lhs_dot(tri_ref[...], ld, 2)
    yield "prologue"
    a = jax.nn.sigmoid(a0 + lora_dot(xa, 1))
    g = lora_dot(jax.nn.sigmoid(xg), 2)
    yield "prologue"
    kk = k * k_k
    kk = kk / jnp.maximum(jnp.sqrt(head_sum(kk * kk, 1)), 1e-12)
    yield "prologue"
    kp = k * (1.0 + (a - 1.0) * k_a)
    avec = -kk
    bvec = kk * a
    yield "prologue"

    lane = lax.broadcasted_iota(jnp.int32, (1, 128), 1)
    m0 = lane < HEAD_DIM
    ri = lax.broadcasted_iota(jnp.int32, (CHUNK, 128), 0)
    ci = lax.broadcasted_iota(jnp.int32, (CHUNK, 128), 1) % HEAD_DIM
    strict = ci < ri
    incl = ci <= ri
    eye = ci == ri
    blk16 = (ci // 16) == (ri // 16)
    blk32 = (ci // 32) == (ri // 32)

    units = [(slice(c * CHUNK, (c + 1) * CHUNK), slice(p * 128, (p + 1) * 128))
             for c in range(nchunk) for p in range(npair)]
    each = lambda fn, *lists: [fn(*args) for args in zip(*lists)]
    cat0 = lambda *xs: jnp.concatenate(xs, axis=0)
    cat1 = lambda *xs: jnp.concatenate(xs, axis=1)

    def prep(u):
        rows, cols = u
        cm = cum[rows, cols]
        c_last = cm[CHUNK - 1:CHUNK, :]
        w_in = jnp.exp(cm)
        w_ex = jnp.exp(cm - ld[rows, cols])
        w_inv = jnp.exp(-cm)
        w_out = jnp.exp(c_last - cm)
        r_f = r[rows, cols] * w_in
        return dict(
            r_f=r_f, r_t=r_f.astype(BF16),
            a_t=(avec[rows, cols] * w_ex).astype(BF16),
            b_t=(bvec[rows, cols] * w_inv).astype(BF16),
            k_t=(kp[rows, cols] * w_inv).astype(BF16),
            b_o=(bvec[rows, cols] * w_out).astype(BF16),
            k_o=(kp[rows, cols] * w_out).astype(BF16),
            vv=v[rows, cols].astype(BF16),
            wc=jnp.where(eye, jnp.exp(c_last), 0.0))

    P = []
    for c in range(nchunk):
        P += each(prep, units[c * npair:(c + 1) * npair])
        yield "prologue"
    yield "prologue done"

    sc = each(lambda d: _dot_nt(cat0(d["a_t"], d["r_t"]), cat0(_bd(d["b_t"]), _bd(d["k_t"]))), P)
    yield "main"
    l_ab = [jnp.where(strict, x[0:CHUNK, 0:128], 0.0) for x in sc]
    m_ak = [jnp.where(strict, x[0:CHUNK, 128:256], 0.0) for x in sc]
    m_rb = [jnp.where(incl, x[CHUNK:, 0:128], 0.0).astype(BF16) for x in sc]
    m_rk = [jnp.where(incl, x[CHUNK:, 128:256], 0.0) for x in sc]

    mv = each(lambda ak, rk, d: _dot(cat0(ak, rk).astype(BF16), _bd(d["vv"])), m_ak, m_rk, P)
    yield "main"
    makv = [x[0:CHUNK].astype(BF16) for x in mv]
    mrkv = [x[CHUNK:] for x in mv]

    pw = [jnp.where(blk16, x, 0.0).astype(BF16) for x in l_ab]
    tmat = [jnp.where(eye, 1.0, 0.0) for _ in units]
    for it in range(4):
        if it < 3:
            z = each(lambda a, t: _dot(a, _bd(cat1(a, t.astype(BF16)))), pw, tmat)
            pw = [x[:, 0:128].astype(BF16) for x in z]
            tmat = each(lambda t, x: t + x[:, 128:256], tmat, z)
        else:
            tmat = each(lambda a, t: t + _dot(a, _bd(t.astype(BF16))), pw, tmat)
        yield "main"
    for sel in (lambda x: jnp.where(blk32 & ~blk16, x, 0.0), lambda x: jnp.where(blk32, 0.0, x)):
        off = [sel(x).astype(BF16) for x in l_ab]
        t16 = [t.astype(BF16) for t in tmat]
        ot = each(lambda o, t: _dot(o, _bd(t)).astype(BF16), off, t16)
        yield "main"
        tmat = each(lambda t, t6, o: t + _dot(t6, _bd(o)), tmat, t16, ot)
        yield "main"

    xh = each(lambda t, d, m: _dot(t.astype(BF16), _bd(cat1(d["a_t"], m))), tmat, P, makv)
    yield "main"
    ahv = [x.astype(BF16) for x in xh]

    mr = each(lambda m, x: _dot(m, _bd(x)), m_rb, ahv)
    yield "main"
    r_hat = each(lambda d, x: d["r_f"] + x[:, 0:128], P, mr)
    y_intra = each(lambda x, y: x[:, 128:256] + y, mr, mrkv)

    gh = each(lambda d, x: _dot_tn(cat0(d["b_o"], d["k_o"]),
                                   cat0(x, cat1(jnp.zeros_like(d["vv"]), d["vv"]))), P, ahv)
    gt = each(lambda x, d: jnp.where(m0, x[0:CHUNK, 0:128], x[CHUNK:, 0:128]) + d["wc"], gh, P)
    ht = [jnp.where(m0, x[0:CHUNK, 128:256], x[CHUNK:, 128:256]) for x in gh]
    lhs = each(lambda a, b: cat0(a, b).astype(BF16), r_hat, gt)
    yield "main done"

    st = st_box[0]
    for c in range(nchunk):
        sl = slice(c * npair, (c + 1) * npair)
        ys = each(lambda a, s_: _dot(a, _bd(s_.astype(BF16))), lhs[sl], st)
        for (rows, cols), y_, yi in zip(units[sl], ys, y_intra[sl]):
            y_scr[base + rows.start:base + rows.stop, cols] = y_[0:CHUNK] + yi
        st = each(lambda y_, h_: y_[CHUNK:] + h_, ys, ht[sl])
        yield "state pass"
    st_box[0] = st

    y = y_scr[span, :]
    inv_n = 1.0 / HEAD_DIM
    mean = head_sum(y, 2) * inv_n
    yield "epilogue"
    yc = y - mean
    var = head_sum(yc * yc, 1) * inv_n
    yield "epilogue"
    yn = yc * lax.rsqrt(var + RWKV_GN_EPS) * ln_w + ln_b
    bonus = head_sum(r * kp * r_k, 1) * v
    o_ref[span, :] = ((yn + bonus) * g).astype(o_ref.dtype)


def _rwkv(rkv, lora, lora_w, vecs, tt):
    B, S, _ = rkv.shape
    ones = jnp.asarray(np.kron(np.eye(MXU_DIM // HEAD_DIM), np.ones((HEAD_DIM, HEAD_DIM))), BF16)
    tri = jnp.asarray(np.kron(np.eye(RWKV_SUB // CHUNK), np.tril(np.ones((CHUNK, CHUNK)))), BF16)
    row = lambda w: pl.BlockSpec((None, tt, w), lambda b, s: (b, s, 0))
    return pl.pallas_call(
        _rwkv_kernel,
        grid=(B, S // tt),
        in_specs=[row(rkv.shape[2]), row(lora.shape[2]), _const_spec(lora_w.shape),
                  _const_spec(vecs.shape), _const_spec(ones.shape), _const_spec(tri.shape)],
        out_specs=row(RWKV_W),
        out_shape=jax.ShapeDtypeStruct((B, S, RWKV_W), BF16),
        scratch_shapes=[pltpu.VMEM((RWKV_W // 128, HEAD_DIM, 128), F32),
                        pltpu.VMEM((tt, RWKV_W), F32)],
        compiler_params=_cparams(("arbitrary", "arbitrary")),
        name="rwkv7",
    )(rkv, lora, lora_w, vecs, ones, tri)


def _tail_kernel(x_ref, o0, o1, o2, l0, l1, l2, orw_ref, gate_ref, p_ref,
                 wau_ref, wru_ref, wout_ref, gpost_ref, gpre_ref, wfi_ref, wfo_ref, gffn_ref,
                 wple_ref, wpg_ref, gple_ref, out_ref):
    full = lambda ref: jnp.concatenate([ref[c] for c in range(ref.shape[0])], axis=1)
    ls = [full(l0), full(l1), full(l2)]
    m = jnp.maximum(jnp.maximum(ls[0], ls[1]), ls[2])
    ws = [jnp.exp(l - m) for l in ls]
    o_attn = (ws[0] * full(o0) + ws[1] * full(o1) + ws[2] * full(o2)) / (ws[0] + ws[1] + ws[2])
    d = x_ref.shape[1]
    ga = gate_ref[:, 0:d].astype(F32)
    gr = gate_ref[:, d:2 * d].astype(F32)
    merged = ga * _dot(o_attn.astype(BF16), wau_ref[...]) + gr * _dot(orw_ref[...], wru_ref[...])
    mo = _dot(merged.astype(BF16), wout_ref[...])
    h = x_ref[...] + _rms(mo, gpost_ref[...])

    f = _rms(h, gpre_ref[...]).astype(BF16)
    dff = wfi_ref.shape[1]
    nck = 4
    ck = dff // nck
    acc = jnp.zeros(h.shape, F32)
    for j in range(nck):
        a = jnp.maximum(_dot(f, wfi_ref[:, j * ck:(j + 1) * ck]), 0.0)
        acc = acc + _dot((a * a).astype(BF16), wfo_ref[j * ck:(j + 1) * ck, :])
    h = h + _rms(acc, gffn_ref[...])
    e = _dot(p_ref[...].astype(BF16), wple_ref[...]) * jax.nn.sigmoid(
        _dot(h.astype(BF16), wpg_ref[...]))
    out_ref[...] = h + _rms(e, gple_ref[...])


def _tail(x2, os_, ls_, orw, gates, p3, layer, weights, tm):
    T, D = x2.shape
    row = lambda w: pl.BlockSpec((tm, w), lambda i: (i, 0))
    per_b = os_[0].shape[2] // tm
    attn = pl.BlockSpec((None, ATTN_W // 128, tm, 128), lambda i: (i // per_b, 0, i % per_b, 0))
    p_spec = pl.BlockSpec((None, tm, p3.shape[2]), lambda i: (layer, i, 0))
    return pl.pallas_call(
        _tail_kernel,
        grid=(T // tm,),
        in_specs=[row(D)] + [attn] * 6 + [row(RWKV_W), row(2 * D), p_spec]
                 + [_const_spec(w.shape) for w in weights],
        out_specs=row(D),
        out_shape=jax.ShapeDtypeStruct((T, D), F32),
        compiler_params=_cparams(("arbitrary",)),
        name="tail",
    )(x2, *os_, *ls_, orw, gates, p3, *weights)


def _rope_tables(S):
    half = HEAD_DIM // 2
    inv_freq = ROPE_THETA ** (-jnp.arange(0, HEAD_DIM, 2, dtype=F32) / HEAD_DIM)
    ang = jnp.arange(S, dtype=F32)[:, None] * inv_freq[None, :]
    c, s = jnp.cos(ang), jnp.sin(ang)
    cos_t = jnp.tile(jnp.concatenate([c, c], axis=1), (1, 2))
    sin_t = jnp.tile(jnp.concatenate([-s, s], axis=1), (1, 2))
    scale = HEAD_DIM ** -0.5
    return jnp.concatenate([cos_t * scale, sin_t * scale, cos_t, sin_t], axis=1)


def _pad_slots(parts, axis):
    out = []
    for a in parts:
        pad = [(0, 0)] * a.ndim
        pad[axis] = (0, LORA_SLOT - a.shape[axis])
        out.append(jnp.pad(a, pad))
    return jnp.concatenate(out, axis=axis)


def kernel(x, p, mix_pre_norm, w_in, rwkv_mu, rwkv_w0, rwkv_w2, rwkv_a0, rwkv_a2, rwkv_g2,
           rwkv_k_k, rwkv_k_a, rwkv_r_k, rwkv_ln_w, rwkv_ln_b, w_attn_up, w_rwkv_up, w_out,
           mix_post_norm, ffn_pre_norm, w_ff_in, w_ff_out, ffn_post_norm, w_ple, w_ple_gate,
           ple_post_norm):
    B, S, D = x.shape
    depth = w_in.shape[0]
    nqkv = 3 * N_GROUPS * ATTN_W
    nr = 3 * RWKV_W
    lo = nqkv + nr
    h = x
    rope = _rope_tables(S)
    for i in range(depth):
        w = w_in[i]
        wl = _pad_slots([w[:, lo:lo + DECAY_LORA],
                         w[:, lo + DECAY_LORA:lo + DECAY_LORA + ICLR_LORA],
                         w[:, lo + DECAY_LORA + ICLR_LORA:lo + DECAY_LORA + ICLR_LORA + GATE_LORA]], 1)
        ng = lo + DECAY_LORA + ICLR_LORA + GATE_LORA
        mu = rwkv_mu[i]
        mul = _pad_slots([mu[nr:nr + DECAY_LORA], mu[nr + DECAY_LORA:nr + DECAY_LORA + ICLR_LORA],
                          mu[nr + DECAY_LORA + ICLR_LORA:]], 0)
        q0, q1, q2, rkv, lora, gates = _in_proj(
            h, mix_pre_norm[i][None, :], w[:, :nqkv].astype(BF16),
            w[:, nqkv:lo].astype(BF16), wl.astype(BF16), w[:, ng:].astype(BF16), rope,
            mu[None, :nr], mul[None, :], tm=min(512, S))

        os_, ls_ = [], []
        for qkv_g, (_, dil) in zip((q0, q1, q2), ATTN_GROUPS):
            o, lse = _attn_group(qkv_g, dil, tok=min(4096, S))
            os_.append(o)
            ls_.append(lse)

        lora_w = _pad_slots([rwkv_w2[i], rwkv_a2[i], rwkv_g2[i]], 0).astype(BF16)
        vecs = jnp.stack([rwkv_w0[i], rwkv_a0[i], rwkv_k_k[i], rwkv_k_a[i],
                          rwkv_r_k[i].reshape(-1), rwkv_ln_w[i], rwkv_ln_b[i],
                          jnp.zeros((RWKV_W,), F32)])
        orw = _rwkv(rkv, lora, lora_w, vecs, tt=min(1024, S))

        tm = min(512, S)
        weights = (w_attn_up[i].astype(BF16), w_rwkv_up[i].astype(BF16), w_out[i].astype(BF16),
                   mix_post_norm[i][None, :], ffn_pre_norm[i][None, :], w_ff_in[i].astype(BF16),
                   w_ff_out[i].astype(BF16), ffn_post_norm[i][None, :], w_ple[i].astype(BF16),
                   w_ple_gate[i].astype(BF16), ple_post_norm[i][None, :])
        h2 = _tail(h.reshape(B * S, D), os_, ls_, orw.reshape(B * S, RWKV_W),
                   gates.reshape(B * S, 2 * D), p.reshape(depth, B * S, -1), i, weights, tm)
        h = h2.reshape(B, S, D)
    return h
```

```python
import functools

import numpy as np
import jax
import jax.numpy as jnp
from jax import lax
from jax.experimental import pallas as pl
from jax.experimental.pallas import tpu as pltpu

F32 = jnp.float32
BF16 = jnp.bfloat16

HEAD_DIM = 64
ATTN_GROUPS = ((128, 1), (512, 4), (2048, 16))
N_GROUPS = 3
ATTN_W = 256
ATTN_KEYS = 128
ROPE_THETA = 10000.0
RWKV_W = 512
RWKV_HEADS = 8
DECAY_LORA, ICLR_LORA, GATE_LORA = 32, 32, 96
LORA_SLOT = 128
RWKV_GN_EPS = 64e-5
NORM_EPS = 1e-6
CHUNK = 64
RWKV_SUB = 256
NEG = -1e30

VMEM_LIMIT = 56 * 1024 * 1024
MXU_DIM = 256


def _cparams(sem):
    return pltpu.CompilerParams(dimension_semantics=sem, vmem_limit_bytes=VMEM_LIMIT)


def _const_spec(shape):
    nd = len(shape)
    return pl.BlockSpec(shape, lambda *_: (0,) * nd, pipeline_mode=pl.Buffered(1))


def _dot(a, b):
    return jnp.dot(a, b, preferred_element_type=F32)


def _dot_nt(a, b):
    return lax.dot_general(a, b, (((1,), (1,)), ((), ())), preferred_element_type=F32)


def _dot_tn(a, b):
    return lax.dot_general(a, b, (((0,), (0,)), ((), ())), preferred_element_type=F32)


def _split3(x):
    hi = x.astype(BF16)
    r1 = x - hi.astype(F32)
    mid = r1.astype(BF16)
    lo = (r1 - mid.astype(F32)).astype(BF16)
    return hi, mid, lo


def _dot_exact_rhs(x, rhs_bf16, passes):
    parts = _split3(x)[:passes]
    acc = _dot(parts[0], rhs_bf16)
    for p in parts[1:]:
        acc = acc + _dot(p, rhs_bf16)
    return acc


def _exact_lhs_dot(lhs_bf16, x, passes):
    parts = _split3(x)[:passes]
    acc = _dot(lhs_bf16, parts[0])
    for p in parts[1:]:
        acc = acc + _dot(lhs_bf16, p)
    return acc


def _rms(x, g):
    ms = jnp.mean(x * x, axis=-1, keepdims=True)
    return x * lax.rsqrt(ms + NORM_EPS) * g


def _in_proj_kernel(x_ref, g_ref, wqkv_ref, wr_ref, wl_ref, wg_ref, rope_ref,
                    mur_ref, mul_ref, qkv0_ref, qkv1_ref, qkv2_ref, rkv_ref, lora_ref, gate_ref,
                    carry_r, carry_l, qkv_scr):
    s = pl.program_id(1)
    tm = x_ref.shape[0]

    @pl.when(s == 0)
    def _():
        carry_r[...] = jnp.zeros_like(carry_r)
        carry_l[...] = jnp.zeros_like(carry_l)

    u = _rms(x_ref[...], g_ref[...]).astype(BF16)
    gate_ref[...] = jax.nn.sigmoid(_dot(u, wg_ref[...])).astype(BF16)

    row0 = lax.broadcasted_iota(jnp.int32, (tm, 1), 0) == 0

    def shift_lerp(z_, carry, mu):
        zs = jnp.where(row0, carry[0:1, :], pltpu.roll(z_, 1, 0))
        carry[0:1, :] = z_[tm - 1:tm, :]
        return z_ + (zs - z_) * mu

    rkv_ref[...] = shift_lerp(_dot(u, wr_ref[...]), carry_r, mur_ref[...])

    z = _dot(u, wqkv_ref[...])
    lane = lax.broadcasted_iota(jnp.int32, (1, 128), 1)
    first_half = (lane % HEAD_DIM) < (HEAD_DIM // 2)
    gw = 3 * ATTN_W
    per_group = gw // 128
    halves = ATTN_W // 128
    for c in range(z.shape[1] // 128):
        typ, rest = divmod(c, N_GROUPS * halves)
        g, half = divmod(rest, halves)
        dst = typ * halves + half
        xc = z[:, c * 128:(c + 1) * 128]
        if typ < 2:
            t0 = 0 if typ == 0 else 256
            rot = jnp.where(first_half, pltpu.roll(xc, 96, 1), pltpu.roll(xc, 32, 1))
            xc = xc * rope_ref[:, t0:t0 + 128] + rot * rope_ref[:, t0 + 128:t0 + 256]
        if g == 0:
            qkv0_ref[0, :, dst * 128:(dst + 1) * 128] = xc.astype(BF16)
        else:
            qkv_scr[(g - 1) * per_group + dst] = xc
    for g, out_ref in ((1, qkv1_ref), (2, qkv2_ref)):
        dil = out_ref.shape[0]
        for r in range(dil):
            for cb in range(per_group):
                out_ref[r, :, cb * 128:(cb + 1) * 128] = qkv_scr[
                    (g - 1) * per_group + cb, pl.ds(r, tm // dil, stride=dil), :].astype(BF16)

    lora_ref[...] = shift_lerp(_dot(u, wl_ref[...]), carry_l, mul_ref[...])


def _in_proj(x, g, wqkv, wr, wl, wg, rope, mur, mul, tm):
    B, S, D = x.shape
    gw = 3 * ATTN_W
    row = lambda w: pl.BlockSpec((None, tm, w), lambda b, s: (b, s, 0))
    dils = [d for _, d in ATTN_GROUPS]
    qkv_specs = [pl.BlockSpec((None, d, tm // d, gw), lambda b, s: (b, 0, s, 0)) for d in dils]
    qkv_shapes = [jax.ShapeDtypeStruct((B, d, S // d, gw), BF16) for d in dils]
    return pl.pallas_call(
        _in_proj_kernel,
        grid=(B, S // tm),
        in_specs=[row(D), _const_spec(g.shape), _const_spec(wqkv.shape),
                  _const_spec(wr.shape), _const_spec(wl.shape), _const_spec(wg.shape),
                  pl.BlockSpec((tm, rope.shape[1]), lambda b, s: (s, 0)),
                  _const_spec(mur.shape), _const_spec(mul.shape)],
        out_specs=qkv_specs + [row(wr.shape[1]), row(wl.shape[1]), row(wg.shape[1])],
        out_shape=qkv_shapes + [jax.ShapeDtypeStruct((B, S, wr.shape[1]), F32),
                                jax.ShapeDtypeStruct((B, S, wl.shape[1]), F32),
                                jax.ShapeDtypeStruct((B, S, wg.shape[1]), BF16)],
        scratch_shapes=[pltpu.VMEM((8, wr.shape[1]), F32), pltpu.VMEM((8, wl.shape[1]), F32),
                        pltpu.VMEM(((N_GROUPS - 1) * gw // 128, tm, 128), F32)],
        compiler_params=_cparams(("arbitrary", "arbitrary")),
        name="in_proj",
    )(x, g, wqkv, wr, wl, wg, rope, mur, mul)


def _attn_kernel(cur_ref, kp_ref, vp_ref, o_ref, lse_ref, *, dil, nsub):
    i = pl.program_id(1)
    kw = 2 * ATTN_KEYS
    col = lax.broadcasted_iota(jnp.int32, (ATTN_KEYS, kw), 1)
    rowi = lax.broadcasted_iota(jnp.int32, (ATTN_KEYS, kw), 0)
    diff = col - rowi
    band = (diff >= 0) & (diff <= ATTN_KEYS)
    lane = lax.broadcasted_iota(jnp.int32, (1, ATTN_W), 1)
    hmask = [(lane // HEAD_DIM) == h for h in range(ATTN_W // HEAD_DIM)]
    zero_q = jnp.zeros((ATTN_KEYS, ATTN_W), BF16)
    low_half = lax.broadcasted_iota(jnp.int32, (1, 128), 1) < HEAD_DIM
    qs, ks, vs = slice(0, ATTN_W), slice(ATTN_W, 2 * ATTN_W), slice(2 * ATTN_W, 3 * ATTN_W)

    def load(u):
        r, sb = (u, 0) if nsub == 1 else ((0, u) if dil == 1 else (u // nsub, u % nsub))
        rows = pl.ds(pl.multiple_of(sb * ATTN_KEYS, ATTN_KEYS), ATTN_KEYS)
        q = cur_ref[r, rows, qs]
        if nsub == 1:
            k_lo, v_lo = kp_ref[r], vp_ref[r]
        else:
            lo = pl.ds(pl.multiple_of(jnp.maximum(sb - 1, 0) * ATTN_KEYS, ATTN_KEYS), ATTN_KEYS)
            first = sb == 0
            k_lo = jnp.where(first, kp_ref[r], cur_ref[r, lo, ks])
            v_lo = jnp.where(first, vp_ref[r], cur_ref[r, lo, vs])
        k = jnp.concatenate([k_lo, cur_ref[r, rows, ks]], axis=0)
        v = jnp.concatenate([v_lo, cur_ref[r, rows, vs]], axis=0)
        valid = band & ((col >= ATTN_KEYS) | (i > 0) | (sb > 0))
        return r, sb, q, k, v, valid

    def body(j, carry):
        us = [load(2 * j), load(2 * j + 1)]
        scs = [[jnp.where(valid, _dot_nt(jnp.where(hm, q, zero_q), k), NEG) for hm in hmask]
               for (_, _, q, k, _, valid) in us]
        ms = [[jnp.max(sc, axis=-1, keepdims=True) for sc in row] for row in scs]
        es = [[jnp.exp(sc - m) for sc, m in zip(r1, r2)] for r1, r2 in zip(scs, ms)]
        dens = [[jnp.sum(e, axis=-1, keepdims=True) for e in row] for row in es]
        ohs = [[_dot(e.astype(BF16), u_[4]) for e in row] for row, u_ in zip(es, us)]
        for (r, sb, *_), oh_r, den_r, m_r in zip(us, ohs, dens, ms):
            if dil == 1:
                dst = pl.ds(pl.multiple_of(sb * ATTN_KEYS, ATTN_KEYS), ATTN_KEYS)
            else:
                dst = pl.ds(r + sb * (ATTN_KEYS * dil), ATTN_KEYS, stride=dil)
            for cb in range(ATTN_W // 128):
                cols = slice(cb * 128, (cb + 1) * 128)
                h0, h1 = 2 * cb, 2 * cb + 1
                o_ref[cb, dst, :] = jnp.where(low_half, oh_r[h0][:, cols] / den_r[h0],
                                              oh_r[h1][:, cols] / den_r[h1])
                lse_ref[cb, dst, :] = jnp.where(low_half, m_r[h0] + jnp.log(den_r[h0]),
                                                m_r[h1] + jnp.log(den_r[h1]))
        return carry

    lax.fori_loop(0, dil * nsub // 2, body, 0)


def _attn_group(qkv_g, dil, tok):
    B, _, n, gw = qkv_g.shape
    S = n * dil
    tq = tok // dil
    nsub = tq // ATTN_KEYS
    cur = pl.BlockSpec((None, dil, tq, gw), lambda b, i: (b, 0, i, 0))
    prev = lambda c: pl.BlockSpec((None, dil, ATTN_KEYS, ATTN_W),
                                  lambda b, i: (b, 0, jnp.maximum(i * nsub - 1, 0), c))
    out = pl.BlockSpec((None, ATTN_W // 128, tok, 128), lambda b, i: (b, 0, i, 0))
    return pl.pallas_call(
        functools.partial(_attn_kernel, dil=dil, nsub=nsub),
        grid=(B, S // tok),
        in_specs=[cur, prev(1), prev(2)],
        out_specs=[out, out],
        out_shape=[jax.ShapeDtypeStruct((B, ATTN_W // 128, S, 128), F32)] * 2,
        compiler_params=_cparams(("arbitrary", "arbitrary")),
        name=f"attn_d{dil}",
    )(qkv_g, qkv_g, qkv_g)


def _bd(x):
    lane = lax.broadcasted_iota(jnp.int32, (1, x.shape[1]), 1)
    m0 = (lane % 128) < HEAD_DIM
    z = jnp.zeros_like(x)
    return jnp.concatenate([jnp.where(m0, x, z), jnp.where(m0, z, x)], axis=0)


def _rwkv_kernel(rkv_ref, lora_ref, lw_ref, vec_ref, ones_ref, tri_ref,
                 o_ref, st_ref, y_scr):
    s = pl.program_id(1)
    nsub = rkv_ref.shape[0] // RWKV_SUB

    @pl.when(s == 0)
    def _():
        st_ref[...] = jnp.zeros_like(st_ref)

    st_box = [[st_ref[p] for p in range(RWKV_W // 128)]]
    subs = [_rwkv_sub(i * RWKV_SUB, rkv_ref, lora_ref, lw_ref, vec_ref, ones_ref, tri_ref, o_ref,
                      y_scr, st_box) for i in range(nsub)]

    def advance_to(gen, tag):
        while next(gen) != tag:
            pass

    advance_to(subs[0], "prologue done")
    for i in range(nsub):
        main_done, prev_done, next_done = False, i == 0, i + 1 == nsub
        while not (main_done and prev_done and next_done):
            if not main_done:
                main_done = next(subs[i]) == "main done"
            if not prev_done:
                prev_done = next(subs[i - 1], None) is None
            if not next_done:
                next_done = next(subs[i + 1]) == "prologue done"
    for _ in subs[nsub - 1]:
        pass
    for p, st in enumerate(st_box[0]):
        st_ref[p] = st


def _rwkv_sub(base, rkv_ref, lora_ref, lw_ref, vec_ref, ones_ref, tri_ref, o_ref, y_scr, st_box):
    tt = RWKV_SUB
    nchunk = tt // CHUNK
    npair = RWKV_W // 128
    span = slice(base, base + tt)

    w0, a0, k_k, k_a, r_k, ln_w, ln_b = [vec_ref[j:j + 1, :] for j in range(7)]
    r = rkv_ref[span, 0:RWKV_W]
    k = rkv_ref[span, RWKV_W:2 * RWKV_W]
    v = rkv_ref[span, 2 * RWKV_W:3 * RWKV_W]
    xw = lora_ref[span, 0:LORA_SLOT]
    xa = lora_ref[span, LORA_SLOT:2 * LORA_SLOT]
    xg = lora_ref[span, 2 * LORA_SLOT:3 * LORA_SLOT]
    ones = ones_ref[...]
    ow = ones.shape[0]

    def head_sum(x, passes):
        return jnp.concatenate([_dot_exact_rhs(x[:, c:c + ow], ones, passes)
                                for c in range(0, RWKV_W, ow)], axis=1)

    def lora_dot(x, j):
        return _dot(x.astype(BF16), lw_ref[j * LORA_SLOT:(j + 1) * LORA_SLOT, :])

    y_w = -(w0 + lora_dot(jnp.tanh(xw), 0))
    softplus = jnp.maximum(y_w, 0.0) + jnp.log(1.0 + jnp.exp(-jnp.abs(y_w)))
    ld = -jnp.exp(-softplus - 0.5)
    yield "prologue"
    cum = _exact_lhs_dot(tri_ref[...], ld, 2)
    yield "prologue"
    a = jax.nn.sigmoid(a0 + lora_dot(xa, 1))
    g = lora_dot(jax.nn.sigmoid(xg), 2)
    yield "prologue"
    kk = k * k_k
    kk = kk / jnp.maximum(jnp.sqrt(head_sum(kk * kk, 1)), 1e-12)
    yield "prologue"
    kp = k * (1.0 + (a - 1.0) * k_a)
    avec = -kk
    bvec = kk * a
    yield "prologue"

    lane = lax.broadcasted_iota(jnp.int32, (1, 128), 1)
    m0 = lane < HEAD_DIM
    ri = lax.broadcasted_iota(jnp.int32, (CHUNK, 128), 0)
    ci = lax.broadcasted_iota(jnp.int32, (CHUNK, 128), 1) % HEAD_DIM
    strict = ci < ri
    incl = ci <= ri
    eye = ci == ri
    blk16 = (ci // 16) == (ri // 16)
    blk32 = (ci // 32) == (ri // 32)

    units = [(slice(c * CHUNK, (c + 1) * CHUNK), slice(p * 128, (p + 1) * 128))
             for c in range(nchunk) for p in range(npair)]
    each = lambda fn, *lists: [fn(*args) for args in zip(*lists)]
    cat0 = lambda *xs: jnp.concatenate(xs, axis=0)
    cat1 = lambda *xs: jnp.concatenate(xs, axis=1)

    def prep(u):
        rows, cols = u
        cm = cum[rows, cols]
        c_last = cm[CHUNK - 1:CHUNK, :]
        w_in = jnp.exp(cm)
        w_ex = jnp.exp(cm - ld[rows, cols])
        w_inv = jnp.exp(-cm)
        w_out = jnp.exp(c_last - cm)
        r_f = r[rows, cols] * w_in
        return dict(
            r_f=r_f, r_t=r_f.astype(BF16),
            a_t=(avec[rows, cols] * w_ex).astype(BF16),
            b_t=(bvec[rows, cols] * w_inv).astype(BF16),
            k_t=(kp[rows, cols] * w_inv).astype(BF16),
            b_o=(bvec[rows, cols] * w_out).astype(BF16),
            k_o=(kp[rows, cols] * w_out).astype(BF16),
            vv=v[rows, cols].astype(BF16),
            wc=jnp.where(eye, jnp.exp(c_last), 0.0))

    P = []
    for c in range(nchunk):
        P += each(prep, units[c * npair:(c + 1) * npair])
        yield "prologue"
    yield "prologue done"

    sc = each(lambda d: _dot_nt(cat0(d["a_t"], d["r_t"]), cat0(_bd(d["b_t"]), _bd(d["k_t"]))), P)
    yield "main"
    l_ab = [jnp.where(strict, x[0:CHUNK, 0:128], 0.0) for x in sc]
    m_ak = [jnp.where(strict, x[0:CHUNK, 128:256], 0.0) for x in sc]
    m_rb = [jnp.where(incl, x[CHUNK:, 0:128], 0.0).astype(BF16) for x in sc]
    m_rk = [jnp.where(incl, x[CHUNK:, 128:256], 0.0) for x in sc]

    mv = each(lambda ak, rk, d: _dot(cat0(ak, rk).astype(BF16), _bd(d["vv"])), m_ak, m_rk, P)
    yield "main"
    makv = [x[0:CHUNK].astype(BF16) for x in mv]
    mrkv = [x[CHUNK:] for x in mv]

    pw = [jnp.where(blk16, x, 0.0).astype(BF16) for x in l_ab]
    tmat = [jnp.where(eye, 1.0, 0.0) for _ in units]
    for it in range(4):
        if it < 3:
            z = each(lambda a, t: _dot(a, _bd(cat1(a, t.astype(BF16)))), pw, tmat)
            pw = [x[:, 0:128].astype(BF16) for x in z]
            tmat = each(lambda t, x: t + x[:, 128:256], tmat, z)
        else:
            tmat = each(lambda a, t: t + _dot(a, _bd(t.astype(BF16))), pw, tmat)
        yield "main"
    for sel in (lambda x: jnp.where(blk32 & ~blk16, x, 0.0), lambda x: jnp.where(blk32, 0.0, x)):
        off = [sel(x).astype(BF16) for x in l_ab]
        t16 = [t.astype(BF16) for t in tmat]
        ot = each(lambda o, t: _dot(o, _bd(t)).astype(BF16), off, t16)
        yield "main"
        tmat = each(lambda t, t6, o: t + _dot(t6, _bd(o)), tmat, t16, ot)
        yield "main"

    xh = each(lambda t, d, m: _dot(t.astype(BF16), _bd(cat1(d["a_t"], m))), tmat, P, makv)
    yield "main"
    ahv = [x.astype(BF16) for x in xh]

    mr = each(lambda m, x: _dot(m, _bd(x)), m_rb, ahv)
    yield "main"
    r_hat = each(lambda d, x: d["r_f"] + x[:, 0:128], P, mr)
    y_intra = each(lambda x, y: x[:, 128:256] + y, mr, mrkv)

    gh = each(lambda d, x: _dot_tn(cat0(d["b_o"], d["k_o"]),
                                   cat0(x, cat1(jnp.zeros_like(d["vv"]), d["vv"]))), P, ahv)
    gt = each(lambda x, d: jnp.where(m0, x[0:CHUNK, 0:128], x[CHUNK:, 0:128]) + d["wc"], gh, P)
    ht = [jnp.where(m0, x[0:CHUNK, 128:256], x[CHUNK:, 128:256]) for x in gh]
    lhs = each(lambda a, b: cat0(a, b).astype(BF16), r_hat, gt)
    yield "main done"

    st = st_box[0]
    for c in range(nchunk):
        sl = slice(c * npair, (c + 1) * npair)
        ys = each(lambda a, s_: _dot(a, _bd(s_.astype(BF16))), lhs[sl], st)
        for (rows, cols), y_, yi in zip(units[sl], ys, y_intra[sl]):
            y_scr[base + rows.start:base + rows.stop, cols] = y_[0:CHUNK] + yi
        st = each(lambda y_, h_: y_[CHUNK:] + h_, ys, ht[sl])
        yield "state pass"
    st_box[0] = st

    y = y_scr[span, :]
    inv_n = 1.0 / HEAD_DIM
    mean = head_sum(y, 2) * inv_n
    yield "epilogue"
    yc = y - mean
    var = head_sum(yc * yc, 1) * inv_n
    yield "epilogue"
    yn = yc * lax.rsqrt(var + RWKV_GN_EPS) * ln_w + ln_b
    bonus = head_sum(r * kp * r_k, 1) * v
    o_ref[span, :] = ((yn + bonus) * g).astype(o_ref.dtype)


def _rwkv(rkv, lora, lora_w, vecs, tt):
    B, S, _ = rkv.shape
    ones = jnp.asarray(np.kron(np.eye(MXU_DIM // HEAD_DIM), np.ones((HEAD_DIM, HEAD_DIM))), BF16)
    tri = jnp.asarray(np.kron(np.eye(RWKV_SUB // CHUNK), np.tril(np.ones((CHUNK, CHUNK)))), BF16)
    row = lambda w: pl.BlockSpec((None, tt, w), lambda b, s: (b, s, 0))
    return pl.pallas_call(
        _rwkv_kernel,
        grid=(B, S // tt),
        in_specs=[row(rkv.shape[2]), row(lora.shape[2]), _const_spec(lora_w.shape),
                  _const_spec(vecs.shape), _const_spec(ones.shape), _const_spec(tri.shape)],
        out_specs=row(RWKV_W),
        out_shape=jax.ShapeDtypeStruct((B, S, RWKV_W), BF16),
        scratch_shapes=[pltpu.VMEM((RWKV_W // 128, HEAD_DIM, 128), F32),
                        pltpu.VMEM((tt, RWKV_W), F32)],
        compiler_params=_cparams(("arbitrary", "arbitrary")),
        name="rwkv7",
    )(rkv, lora, lora_w, vecs, ones, tri)


def _tail_kernel(x_ref, o0, o1, o2, l0, l1, l2, orw_ref, gate_ref, p_ref,
                 wau_ref, wru_ref, wout_ref, gpost_ref, gpre_ref, wfi_ref, wfo_ref, gffn_ref,
                 wple_ref, wpg_ref, gple_ref, out_ref):
    full = lambda ref: jnp.concatenate([ref[c] for c in range(ref.shape[0])], axis=1)
    ls = [full(l0), full(l1), full(l2)]
    m = jnp.maximum(jnp.maximum(ls[0], ls[1]), ls[2])
    ws = [jnp.exp(l - m) for l in ls]
    o_attn = (ws[0] * full(o0) + ws[1] * full(o1) + ws[2] * full(o2)) / (ws[0] + ws[1] + ws[2])
    d = x_ref.shape[1]
    ga = gate_ref[:, 0:d].astype(F32)
    gr = gate_ref[:, d:2 * d].astype(F32)
    merged = ga * _dot(o_attn.astype(BF16), wau_ref[...]) + gr * _dot(orw_ref[...], wru_ref[...])
    mo = _dot(merged.astype(BF16), wout_ref[...])
    h = x_ref[...] + _rms(mo, gpost_ref[...])

    f = _rms(h, gpre_ref[...]).astype(BF16)
    dff = wfi_ref.shape[1]
    nck = 4
    ck = dff // nck
    acc = jnp.zeros(h.shape, F32)
    for j in range(nck):
        a = jnp.maximum(_dot(f, wfi_ref[:, j * ck:(j + 1) * ck]), 0.0)
        acc = acc + _dot((a * a).astype(BF16), wfo_ref[j * ck:(j + 1) * ck, :])
    h = h + _rms(acc, gffn_ref[...])
    e = _dot(p_ref[...].astype(BF16), wple_ref[...]) * jax.nn.sigmoid(
        _dot(h.astype(BF16), wpg_ref[...]))
    out_ref[...] = h + _rms(e, gple_ref[...])


def _tail(x2, os_, ls_, orw, gates, p3, layer, weights, tm):
    T, D = x2.shape
    row = lambda w: pl.BlockSpec((tm, w), lambda i: (i, 0))
    per_b = os_[0].shape[2] // tm
    attn = pl.BlockSpec((None, ATTN_W // 128, tm, 128), lambda i: (i // per_b, 0, i % per_b, 0))
    p_spec = pl.BlockSpec((None, tm, p3.shape[2]), lambda i: (layer, i, 0))
    return pl.pallas_call(
        _tail_kernel,
        grid=(T // tm,),
        in_specs=[row(D)] + [attn] * 6 + [row(RWKV_W), row(2 * D), p_spec]
                 + [_const_spec(w.shape) for w in weights],
        out_specs=row(D),
        out_shape=jax.ShapeDtypeStruct((T, D), F32),
        compiler_params=_cparams(("arbitrary",)),
        name="tail",
    )(x2, *os_, *ls_, orw, gates, p3, *weights)


def _rope_tables(S):
    inv_freq = ROPE_THETA ** (-np.arange(0, HEAD_DIM, 2, dtype=np.float64) / HEAD_DIM)
    ang = np.arange(S, dtype=np.float64)[:, None] * inv_freq[None, :]
    c, s = np.cos(ang), np.sin(ang)
    cos_t = np.tile(np.concatenate([c, c], axis=1), (1, 2))
    sin_t = np.tile(np.concatenate([-s, s], axis=1), (1, 2))
    scale = HEAD_DIM ** -0.5
    table = np.concatenate([cos_t * scale, sin_t * scale, cos_t, sin_t], axis=1)
    return jnp.asarray(table, F32)


def _pad_slots(parts, axis):
    out = []
    for a in parts:
        pad = [(0, 0)] * a.ndim
        pad[axis] = (0, LORA_SLOT - a.shape[axis])
        out.append(jnp.pad(a, pad))
    return jnp.concatenate(out, axis=axis)


def kernel(x, p, mix_pre_norm, w_in, rwkv_mu, rwkv_w0, rwkv_w2, rwkv_a0, rwkv_a2, rwkv_g2,
           rwkv_k_k, rwkv_k_a, rwkv_r_k, rwkv_ln_w, rwkv_ln_b, w_attn_up, w_rwkv_up, w_out,
           mix_post_norm, ffn_pre_norm, w_ff_in, w_ff_out, ffn_post_norm, w_ple, w_ple_gate,
           ple_post_norm):
    B, S, D = x.shape
    depth = w_in.shape[0]
    nqkv = 3 * N_GROUPS * ATTN_W
    nr = 3 * RWKV_W
    lo = nqkv + nr
    h = x
    rope = _rope_tables(S)
    for i in range(depth):
        w = w_in[i]
        wl = _pad_slots([w[:, lo:lo + DECAY_LORA],
                         w[:, lo + DECAY_LORA:lo + DECAY_LORA + ICLR_LORA],
                         w[:, lo + DECAY_LORA + ICLR_LORA:lo + DECAY_LORA + ICLR_LORA + GATE_LORA]], 1)
        ng = lo + DECAY_LORA + ICLR_LORA + GATE_LORA
        mu = rwkv_mu[i]
        mul = _pad_slots([mu[nr:nr + DECAY_LORA], mu[nr + DECAY_LORA:nr + DECAY_LORA + ICLR_LORA],
                          mu[nr + DECAY_LORA + ICLR_LORA:]], 0)
        q0, q1, q2, rkv, lora, gates = _in_proj(
            h, mix_pre_norm[i][None, :], w[:, :nqkv].astype(BF16),
            w[:, nqkv:lo].astype(BF16), wl.astype(BF16), w[:, ng:].astype(BF16), rope,
            mu[None, :nr], mul[None, :], tm=min(512, S))

        os_, ls_ = [], []
        for qkv_g, (_, dil) in zip((q0, q1, q2), ATTN_GROUPS):
            o, lse = _attn_group(qkv_g, dil, tok=min(4096, S))
            os_.append(o)
            ls_.append(lse)

        lora_w = _pad_slots([rwkv_w2[i], rwkv_a2[i], rwkv_g2[i]], 0).astype(BF16)
        vecs = jnp.stack([rwkv_w0[i], rwkv_a0[i], rwkv_k_k[i], rwkv_k_a[i],
                          rwkv_r_k[i].reshape(-1), rwkv_ln_w[i], rwkv_ln_b[i],
                          jnp.zeros((RWKV_W,), F32)])
        orw = _rwkv(rkv, lora, lora_w, vecs, tt=min(1024, S))

        tm = min(512, S)
        weights = (w_attn_up[i].astype(BF16), w_rwkv_up[i].astype(BF16), w_out[i].astype(BF16),
                   mix_post_norm[i][None, :], ffn_pre_norm[i][None, :], w_ff_in[i].astype(BF16),
                   w_ff_out[i].astype(BF16), ffn_post_norm[i][None, :], w_ple[i].astype(BF16),
                   w_ple_gate[i].astype(BF16), ple_post_norm[i][None, :])
        h2 = _tail(h.reshape(B * S, D), os_, ls_, orw.reshape(B * S, RWKV_W),
                   gates.reshape(B * S, 2 * D), p.reshape(depth, B * S, -1), i, weights, tm)
        h = h2.reshape(B, S, D)
    return h
```

```python
import functools

import numpy as np
import jax
import jax.numpy as jnp
from jax import lax
from jax.experimental import pallas as pl
from jax.experimental.pallas import tpu as pltpu

F32 = jnp.float32
BF16 = jnp.bfloat16

HEAD_DIM = 64
ATTN_GROUPS = ((128, 1), (512, 4), (2048, 16))
N_GROUPS = 3
ATTN_W = 256
ATTN_KEYS = 128
ROPE_THETA = 10000.0
RWKV_W = 512
DECAY_LORA, ICLR_LORA, GATE_LORA = 32, 32, 96
LORA_SLOT = 128
RWKV_GN_EPS = 64e-5
NORM_EPS = 1e-6
CHUNK = 64
RWKV_SUB = 256
NEG = -1e30

VMEM_LIMIT = 56 * 1024 * 1024
MXU_DIM = 256


def _cparams(sem):
    return pltpu.CompilerParams(dimension_semantics=sem, vmem_limit_bytes=VMEM_LIMIT)


def _const_spec(shape):
    nd = len(shape)
    return pl.BlockSpec(shape, lambda *_: (0,) * nd, pipeline_mode=pl.Buffered(1))


def _dot(a, b):
    return jnp.dot(a, b, preferred_element_type=F32)


def _dot_nt(a, b):
    return lax.dot_general(a, b, (((1,), (1,)), ((), ())), preferred_element_type=F32)


def _dot_tn(a, b):
    return lax.dot_general(a, b, (((0,), (0,)), ((), ())), preferred_element_type=F32)


def _split3(x):
    hi = x.astype(BF16)
    r1 = x - hi.astype(F32)
    mid = r1.astype(BF16)
    lo = (r1 - mid.astype(F32)).astype(BF16)
    return hi, mid, lo


def _dot_exact_rhs(x, rhs_bf16, passes):
    parts = _split3(x)[:passes]
    acc = _dot(parts[0], rhs_bf16)
    for p in parts[1:]:
        acc = acc + _dot(p, rhs_bf16)
    return acc


def _exact_lhs_dot(lhs_bf16, x, passes):
    parts = _split3(x)[:passes]
    acc = _dot(lhs_bf16, parts[0])
    for p in parts[1:]:
        acc = acc + _dot(lhs_bf16, p)
    return acc


def _rms(x, g):
    ms = jnp.mean(x * x, axis=-1, keepdims=True)
    return x * lax.rsqrt(ms + NORM_EPS) * g


def _in_proj_kernel(x_ref, g_ref, wqkv_ref, wr_ref, wl_ref, wg_ref, rope_ref,
                    mur_ref, mul_ref, qkv0_ref, qkv1_ref, qkv2_ref, rkv_ref, lora_ref, gate_ref,
                    carry_r, carry_l, qkv_scr):
    s = pl.program_id(1)
    tm = x_ref.shape[0]

    @pl.when(s == 0)
    def _():
        carry_r[...] = jnp.zeros_like(carry_r)
        carry_l[...] = jnp.zeros_like(carry_l)

    u = _rms(x_ref[...], g_ref[...]).astype(BF16)
    gate_ref[...] = jax.nn.sigmoid(_dot(u, wg_ref[...])).astype(BF16)

    row0 = lax.broadcasted_iota(jnp.int32, (tm, 1), 0) == 0

    def shift_lerp(z_, carry, mu):
        zs = jnp.where(row0, carry[0:1, :], pltpu.roll(z_, 1, 0))
        carry[0:1, :] = z_[tm - 1:tm, :]
        return z_ + (zs - z_) * mu

    rkv_ref[...] = shift_lerp(_dot(u, wr_ref[...]), carry_r, mur_ref[...])

    z = _dot(u, wqkv_ref[...])
    lane = lax.broadcasted_iota(jnp.int32, (1, 128), 1)
    first_half = (lane % HEAD_DIM) < (HEAD_DIM // 2)
    gw = 3 * ATTN_W
    per_group = gw // 128
    halves = ATTN_W // 128
    for c in range(z.shape[1] // 128):
        typ, rest = divmod(c, N_GROUPS * halves)
        g, half = divmod(rest, halves)
        dst = typ * halves + half
        xc = z[:, c * 128:(c + 1) * 128]
        if typ < 2:
            t0 = 0 if typ == 0 else 256
            rot = jnp.where(first_half, pltpu.roll(xc, 96, 1), pltpu.roll(xc, 32, 1))
            xc = xc * rope_ref[:, t0:t0 + 128] + rot * rope_ref[:, t0 + 128:t0 + 256]
        if g == 0:
            qkv0_ref[0, :, dst * 128:(dst + 1) * 128] = xc.astype(BF16)
        else:
            qkv_scr[(g - 1) * per_group + dst] = xc
    for g, out_ref in ((1, qkv1_ref), (2, qkv2_ref)):
        dil = out_ref.shape[0]
        for r in range(dil):
            for cb in range(per_group):
                out_ref[r, :, cb * 128:(cb + 1) * 128] = qkv_scr[
                    (g - 1) * per_group + cb, pl.ds(r, tm // dil, stride=dil), :].astype(BF16)

    lora_ref[...] = shift_lerp(_dot(u, wl_ref[...]), carry_l, mul_ref[...])


def _in_proj(x, g, wqkv, wr, wl, wg, rope, mur, mul, tm):
    B, S, D = x.shape
    gw = 3 * ATTN_W
    row = lambda w: pl.BlockSpec((None, tm, w), lambda b, s: (b, s, 0))
    dils = [d for _, d in ATTN_GROUPS]
    qkv_specs = [pl.BlockSpec((None, d, tm // d, gw), lambda b, s: (b, 0, s, 0)) for d in dils]
    qkv_shapes = [jax.ShapeDtypeStruct((B, d, S // d, gw), BF16) for d in dils]
    return pl.pallas_call(
        _in_proj_kernel,
        grid=(B, S // tm),
        in_specs=[row(D), _const_spec(g.shape), _const_spec(wqkv.shape),
                  _const_spec(wr.shape), _const_spec(wl.shape), _const_spec(wg.shape),
                  pl.BlockSpec((tm, rope.shape[1]), lambda b, s: (s, 0)),
                  _const_spec(mur.shape), _const_spec(mul.shape)],
        out_specs=qkv_specs + [row(wr.shape[1]), row(wl.shape[1]), row(wg.shape[1])],
        out_shape=qkv_shapes + [jax.ShapeDtypeStruct((B, S, wr.shape[1]), F32),
                                jax.ShapeDtypeStruct((B, S, wl.shape[1]), F32),
                                jax.ShapeDtypeStruct((B, S, wg.shape[1]), BF16)],
        scratch_shapes=[pltpu.VMEM((8, wr.shape[1]), F32), pltpu.VMEM((8, wl.shape[1]), F32),
                        pltpu.VMEM(((N_GROUPS - 1) * gw // 128, tm, 128), F32)],
        compiler_params=_cparams(("arbitrary", "arbitrary")),
        name="in_proj",
    )(x, g, wqkv, wr, wl, wg, rope, mur, mul)


def _attn_kernel(cur_ref, kp_ref, vp_ref, o_ref, lse_ref, *, dil, nsub):
    i = pl.program_id(1)
    kw = 2 * ATTN_KEYS
    col = lax.broadcasted_iota(jnp.int32, (ATTN_KEYS, kw), 1)
    rowi = lax.broadcasted_iota(jnp.int32, (ATTN_KEYS, kw), 0)
    diff = col - rowi
    band = (diff >= 0) & (diff <= ATTN_KEYS)
    lane = lax.broadcasted_iota(jnp.int32, (1, ATTN_W), 1)
    hmask = [(lane // HEAD_DIM) == h for h in range(ATTN_W // HEAD_DIM)]
    zero_q = jnp.zeros((ATTN_KEYS, ATTN_W), BF16)
    low_half = lax.broadcasted_iota(jnp.int32, (1, 128), 1) < HEAD_DIM
    qs, ks, vs = slice(0, ATTN_W), slice(ATTN_W, 2 * ATTN_W), slice(2 * ATTN_W, 3 * ATTN_W)

    def load(u):
        r, sb = (u, 0) if nsub == 1 else ((0, u) if dil == 1 else (u // nsub, u % nsub))
        rows = pl.ds(pl.multiple_of(sb * ATTN_KEYS, ATTN_KEYS), ATTN_KEYS)
        q = cur_ref[r, rows, qs]
        if nsub == 1:
            k_lo, v_lo = kp_ref[r], vp_ref[r]
        else:
            lo = pl.ds(pl.multiple_of(jnp.maximum(sb - 1, 0) * ATTN_KEYS, ATTN_KEYS), ATTN_KEYS)
            first = sb == 0
            k_lo = jnp.where(first, kp_ref[r], cur_ref[r, lo, ks])
            v_lo = jnp.where(first, vp_ref[r], cur_ref[r, lo, vs])
        k = jnp.concatenate([k_lo, cur_ref[r, rows, ks]], axis=0)
        v = jnp.concatenate([v_lo, cur_ref[r, rows, vs]], axis=0)
        valid = band & ((col >= ATTN_KEYS) | (i > 0) | (sb > 0))
        return r, sb, q, k, v, valid

    def body(j, carry):
        us = [load(2 * j), load(2 * j + 1)]
        scs = [[jnp.where(valid, _dot_nt(jnp.where(hm, q, zero_q), k), NEG) for hm in hmask]
               for (_, _, q, k, _, valid) in us]
        ms = [[jnp.max(sc, axis=-1, keepdims=True) for sc in row] for row in scs]
        es = [[jnp.exp(sc - m) for sc, m in zip(r1, r2)] for r1, r2 in zip(scs, ms)]
        dens = [[jnp.sum(e, axis=-1, keepdims=True) for e in row] for row in es]
        ohs = [[_dot(e.astype(BF16), u_[4]) for e in row] for row, u_ in zip(es, us)]
        for (r, sb, *_), oh_r, den_r, m_r in zip(us, ohs, dens, ms):
            if dil == 1:
                dst = pl.ds(pl.multiple_of(sb * ATTN_KEYS, ATTN_KEYS), ATTN_KEYS)
            else:
                dst = pl.ds(r + sb * (ATTN_KEYS * dil), ATTN_KEYS, stride=dil)
            for cb in range(ATTN_W // 128):
                cols = slice(cb * 128, (cb + 1) * 128)
                h0, h1 = 2 * cb, 2 * cb + 1
                o_ref[cb, dst, :] = jnp.where(low_half, oh_r[h0][:, cols] / den_r[h0],
                                              oh_r[h1][:, cols] / den_r[h1])
                lse_ref[cb, dst, :] = jnp.where(low_half, m_r[h0] + jnp.log(den_r[h0]),
                                                m_r[h1] + jnp.log(den_r[h1]))
        return carry

    lax.fori_loop(0, dil * nsub // 2, body, 0)


def _attn_group(qkv_g, dil, tok):
    B, _, n, gw = qkv_g.shape
    S = n * dil
    tq = tok // dil
    nsub = tq // ATTN_KEYS
    cur = pl.BlockSpec((None, dil, tq, gw), lambda b, i: (b, 0, i, 0))
    prev = lambda c: pl.BlockSpec((None, dil, ATTN_KEYS, ATTN_W),
                                  lambda b, i: (b, 0, jnp.maximum(i * nsub - 1, 0), c))
    out = pl.BlockSpec((None, ATTN_W // 128, tok, 128), lambda b, i: (b, 0, i, 0))
    return pl.pallas_call(
        functools.partial(_attn_kernel, dil=dil, nsub=nsub),
        grid=(B, S // tok),
        in_specs=[cur, prev(1), prev(2)],
        out_specs=[out, out],
        out_shape=[jax.ShapeDtypeStruct((B, ATTN_W // 128, S, 128), F32)] * 2,
        compiler_params=_cparams(("arbitrary", "arbitrary")),
        name=f"attn_d{dil}",
    )(qkv_g, qkv_g, qkv_g)


def _bd(x):
    lane = lax.broadcasted_iota(jnp.int32, (1, x.shape[1]), 1)
    m0 = (lane % 128) < HEAD_DIM
    z = jnp.zeros_like(x)
    return jnp.concatenate([jnp.where(m0, x, z), jnp.where(m0, z, x)], axis=0)


def _rwkv_kernel(rkv_ref, lora_ref, lw_ref, vec_ref, ones_ref, tri_ref,
                 o_ref, st_ref, y_scr):
    s = pl.program_id(1)
    nsub = rkv_ref.shape[0] // RWKV_SUB

    @pl.when(s == 0)
    def _():
        st_ref[...] = jnp.zeros_like(st_ref)

    st_box = [[st_ref[p] for p in range(RWKV_W // 128)]]
    subs = [_rwkv_sub(i * RWKV_SUB, rkv_ref, lora_ref, lw_ref, vec_ref, ones_ref, tri_ref, o_ref,
                      y_scr, st_box) for i in range(nsub)]

    def advance_to(gen, tag):
        while next(gen) != tag:
            pass

    advance_to(subs[0], "prologue done")
    for i in range(nsub):
        main_done, prev_done, next_done = False, i == 0, i + 1 == nsub
        while not (main_done and prev_done and next_done):
            if not main_done:
                main_done = next(subs[i]) == "main done"
            if not prev_done:
                prev_done = next(subs[i - 1], None) is None
            if not next_done:
                next_done = next(subs[i + 1]) == "prologue done"
    for _ in subs[nsub - 1]:
        pass
    for p, st in enumerate(st_box[0]):
        st_ref[p] = st


def _rwkv_sub(base, rkv_ref, lora_ref, lw_ref, vec_ref, ones_ref, tri_ref, o_ref, y_scr, st_box):
    tt = RWKV_SUB
    nchunk = tt // CHUNK
    npair = RWKV_W // 128
    span = slice(base, base + tt)

    w0, a0, k_k, k_a, r_k, ln_w, ln_b = [vec_ref[j:j + 1, :] for j in range(7)]
    r = rkv_ref[span, 0:RWKV_W]
    k = rkv_ref[span, RWKV_W:2 * RWKV_W]
    v = rkv_ref[span, 2 * RWKV_W:3 * RWKV_W]
    xw = lora_ref[span, 0:LORA_SLOT]
    xa = lora_ref[span, LORA_SLOT:2 * LORA_SLOT]
    xg = lora_ref[span, 2 * LORA_SLOT:3 * LORA_SLOT]
    ones = ones_ref[...]
    ow = ones.shape[0]

    def head_sum(x, passes):
        return jnp.concatenate([_dot_exact_rhs(x[:, c:c + ow], ones, passes)
                                for c in range(0, RWKV_W, ow)], axis=1)

    def lora_dot(x, j):
        return _dot(x.astype(BF16), lw_ref[j * LORA_SLOT:(j + 1) * LORA_SLOT, :])

    y_w = -(w0 + lora_dot(jnp.tanh(xw), 0))
    softplus = jnp.maximum(y_w, 0.0) + jnp.log(1.0 + jnp.exp(-jnp.abs(y_w)))
    ld = -jnp.exp(-softplus - 0.5)
    yield "prologue"
    cum = _exact_lhs_dot(tri_ref[...], ld, 2)
    yield "prologue"
    a = jax.nn.sigmoid(a0 + lora_dot(xa, 1))
    g = lora_dot(jax.nn.sigmoid(xg), 2)
    yield "prologue"
    kk = k * k_k
    kk = kk / jnp.maximum(jnp.sqrt(head_sum(kk * kk, 1)), 1e-12)
    yield "prologue"
    kp = k * (1.0 + (a - 1.0) * k_a)
    avec = -kk
    bvec = kk * a
    yield "prologue"

    lane = lax.broadcasted_iota(jnp.int32, (1, 128), 1)
    m0 = lane < HEAD_DIM
    ri = lax.broadcasted_iota(jnp.int32, (CHUNK, 128), 0)
    ci = lax.broadcasted_iota(jnp.int32, (CHUNK, 128), 1) % HEAD_DIM
    strict = ci < ri
    incl = ci <= ri
    eye = ci == ri
    blk16 = (ci // 16) == (ri // 16)
    blk32 = (ci // 32) == (ri // 32)

    units = [(slice(c * CHUNK, (c + 1) * CHUNK), slice(p * 128, (p + 1) * 128))
             for c in range(nchunk) for p in range(npair)]
    each = lambda fn, *lists: [fn(*args) for args in zip(*lists)]
    cat0 = lambda *xs: jnp.concatenate(xs, axis=0)
    cat1 = lambda *xs: jnp.concatenate(xs, axis=1)

    def prep(u):
        rows, cols = u
        cm = cum[rows, cols]
        c_last = cm[CHUNK - 1:CHUNK, :]
        w_in = jnp.exp(cm)
        w_ex = jnp.exp(cm - ld[rows, cols])
        w_inv = jnp.exp(-cm)
        w_out = jnp.exp(c_last - cm)
        r_f = r[rows, cols] * w_in
        return dict(
            r_f=r_f, r_t=r_f.astype(BF16),
            a_t=(avec[rows, cols] * w_ex).astype(BF16),
            b_t=(bvec[rows, cols] * w_inv).astype(BF16),
            k_t=(kp[rows, cols] * w_inv).astype(BF16),
            b_o=(bvec[rows, cols] * w_out).astype(BF16),
            k_o=(kp[rows, cols] * w_out).astype(BF16),
            vv=v[rows, cols].astype(BF16),
            wc=jnp.where(eye, jnp.exp(c_last), 0.0))

    P = []
    for c in range(nchunk):
        P += each(prep, units[c * npair:(c + 1) * npair])
        yield "prologue"
    yield "prologue done"

    sc = each(lambda d: _dot_nt(cat0(d["a_t"], d["r_t"]), cat0(_bd(d["b_t"]), _bd(d["k_t"]))), P)
    yield "main"
    l_ab = [jnp.where(strict, x[0:CHUNK, 0:128], 0.0) for x in sc]
    m_ak = [jnp.where(strict, x[0:CHUNK, 128:256], 0.0) for x in sc]
    m_rb = [jnp.where(incl, x[CHUNK:, 0:128], 0.0).astype(BF16) for x in sc]
    m_rk = [jnp.where(incl, x[CHUNK:, 128:256], 0.0) for x in sc]

    mv = each(lambda ak, rk, d: _dot(cat0(ak, rk).astype(BF16), _bd(d["vv"])), m_ak, m_rk, P)
    yield "main"
    makv = [x[0:CHUNK].astype(BF16) for x in mv]
    mrkv = [x[CHUNK:] for x in mv]

    pw = [jnp.where(blk16, x, 0.0).astype(BF16) for x in l_ab]
    tmat = [jnp.where(eye, 1.0, 0.0) for _ in units]
    for it in range(4):
        if it < 3:
            z = each(lambda a, t: _dot(a, _bd(cat1(a, t.astype(BF16)))), pw, tmat)
            pw = [x[:, 0:128].astype(BF16) for x in z]
            tmat = each(lambda t, x: t + x[:, 128:256], tmat, z)
        else:
            tmat = each(lambda a, t: t + _dot(a, _bd(t.astype(BF16))), pw, tmat)
        yield "main"
    for sel in (lambda x: jnp.where(blk32 & ~blk16, x, 0.0), lambda x: jnp.where(blk32, 0.0, x)):
        off = [sel(x).astype(BF16) for x in l_ab]
        t16 = [t.astype(BF16) for t in tmat]
        ot = each(lambda o, t: _dot(o, _bd(t)).astype(BF16), off, t16)
        yield "main"
        tmat = each(lambda t, t6, o: t + _dot(t6, _bd(o)), tmat, t16, ot)
        yield "main"

    xh = each(lambda t, d, m: _dot(t.astype(BF16), _bd(cat1(d["a_t"], m))), tmat, P, makv)
    yield "main"
    ahv = [x.astype(BF16) for x in xh]

    mr = each(lambda m, x: _dot(m, _bd(x)), m_rb, ahv)
    yield "main"
    r_hat = each(lambda d, x: d["r_f"] + x[:, 0:128], P, mr)
    y_intra = each(lambda x, y: x[:, 128:256] + y, mr, mrkv)

    gh = each(lambda d, x: _dot_tn(cat0(d["b_o"], d["k_o"]),
                                   cat0(x, cat1(jnp.zeros_like(d["vv"]), d["vv"]))), P, ahv)
    gt = each(lambda x, d: jnp.where(m0, x[0:CHUNK, 0:128], x[CHUNK:, 0:128]) + d["wc"], gh, P)
    ht = [jnp.where(m0, x[0:CHUNK, 128:256], x[CHUNK:, 128:256]) for x in gh]
    lhs = each(lambda a, b: cat0(a, b).astype(BF16), r_hat, gt)
    yield "main done"

    st = st_box[0]
    for c in range(nchunk):
        sl = slice(c * npair, (c + 1) * npair)
        ys = each(lambda a, s_: _dot(a, _bd(s_.astype(BF16))), lhs[sl], st)
        for (rows, cols), y_, yi in zip(units[sl], ys, y_intra[sl]):
            y_scr[base + rows.start:base + rows.stop, cols] = y_[0:CHUNK] + yi
        st = each(lambda y_, h_: y_[CHUNK:] + h_, ys, ht[sl])
        yield "state pass"
    st_box[0] = st

    y = y_scr[span, :]
    inv_n = 1.0 / HEAD_DIM
    mean = head_sum(y, 2) * inv_n
    yield "epilogue"
    yc = y - mean
    var = head_sum(yc * yc, 1) * inv_n
    yield "epilogue"
    yn = yc * lax.rsqrt(var + RWKV_GN_EPS) * ln_w + ln_b
    bonus = head_sum(r * kp * r_k, 1) * v
    o_ref[span, :] = ((yn + bonus) * g).astype(o_ref.dtype)


def _rwkv(rkv, lora, lora_w, vecs, tt):
    B, S, _ = rkv.shape
    ones = jnp.asarray(np.kron(np.eye(MXU_DIM // HEAD_DIM), np.ones((HEAD_DIM, HEAD_DIM))), BF16)
    tri = jnp.asarray(np.kron(np.eye(RWKV_SUB // CHUNK), np.tril(np.ones((CHUNK, CHUNK)))), BF16)
    row = lambda w: pl.BlockSpec((None, tt, w), lambda b, s: (b, s, 0))
    return pl.pallas_call(
        _rwkv_kernel,
        grid=(B, S // tt),
        in_specs=[row(rkv.shape[2]), row(lora.shape[2]), _const_spec(lora_w.shape),
                  _const_spec(vecs.shape), _const_spec(ones.shape), _const_spec(tri.shape)],
        out_specs=row(RWKV_W),
        out_shape=jax.ShapeDtypeStruct((B, S, RWKV_W), BF16),
        scratch_shapes=[pltpu.VMEM((RWKV_W // 128, HEAD_DIM, 128), F32),
                        pltpu.VMEM((tt, RWKV_W), F32)],
        compiler_params=_cparams(("arbitrary", "arbitrary")),
        name="rwkv7",
    )(rkv, lora, lora_w, vecs, ones, tri)


def _tail_kernel(x_ref, o0, o1, o2, l0, l1, l2, orw_ref, gate_ref, p_ref,
                 wau_ref, wru_ref, wout_ref, gpost_ref, gpre_ref, wfi_ref, wfo_ref, gffn_ref,
                 wple_ref, wpg_ref, gple_ref, out_ref):
    full = lambda ref: jnp.concatenate([ref[c] for c in range(ref.shape[0])], axis=1)
    ls = [full(l0), full(l1), full(l2)]
    m = jnp.maximum(jnp.maximum(ls[0], ls[1]), ls[2])
    ws = [jnp.exp(l - m) for l in ls]
    o_attn = (ws[0] * full(o0) + ws[1] * full(o1) + ws[2] * full(o2)) / (ws[0] + ws[1] + ws[2])
    d = x_ref.shape[1]
    ga = gate_ref[:, 0:d].astype(F32)
    gr = gate_ref[:, d:2 * d].astype(F32)
    merged = ga * _dot(o_attn.astype(BF16), wau_ref[...]) + gr * _dot(orw_ref[...], wru_ref[...])
    mo = _dot(merged.astype(BF16), wout_ref[...])
    h = x_ref[...] + _rms(mo, gpost_ref[...])

    f = _rms(h, gpre_ref[...]).astype(BF16)
    dff = wfi_ref.shape[1]
    nck = 4
    ck = dff // nck
    acc = jnp.zeros(h.shape, F32)
    for j in range(nck):
        a = jnp.maximum(_dot(f, wfi_ref[:, j * ck:(j + 1) * ck]), 0.0)
        acc = acc + _dot((a * a).astype(BF16), wfo_ref[j * ck:(j + 1) * ck, :])
    h = h + _rms(acc, gffn_ref[...])
    e = _dot(p_ref[...].astype(BF16), wple_ref[...]) * jax.nn.sigmoid(
        _dot(h.astype(BF16), wpg_ref[...]))
    out_ref[...] = h + _rms(e, gple_ref[...])


def _tail(x2, os_, ls_, orw, gates, p3, layer, weights, tm):
    T, D = x2.shape
    row = lambda w: pl.BlockSpec((tm, w), lambda i: (i, 0))
    per_b = os_[0].shape[2] // tm
    attn = pl.BlockSpec((None, ATTN_W // 128, tm, 128), lambda i: (i // per_b, 0, i % per_b, 0))
    p_spec = pl.BlockSpec((None, tm, p3.shape[2]), lambda i: (layer, i, 0))
    return pl.pallas_call(
        _tail_kernel,
        grid=(T // tm,),
        in_specs=[row(D)] + [attn] * 6 + [row(RWKV_W), row(2 * D), p_spec]
                 + [_const_spec(w.shape) for w in weights],
        out_specs=row(D),
        out_shape=jax.ShapeDtypeStruct((T, D), F32),
        compiler_params=_cparams(("arbitrary",)),
        name="tail",
    )(x2, *os_, *ls_, orw, gates, p3, *weights)


def _rope_tables(S):
    inv_freq = ROPE_THETA ** (-np.arange(0, HEAD_DIM, 2, dtype=np.float64) / HEAD_DIM)
    ang = np.arange(S, dtype=np.float64)[:, None] * inv_freq[None, :]
    c, s = np.cos(ang), np.sin(ang)
    cos_t = np.tile(np.concatenate([c, c], axis=1), (1, 2))
    sin_t = np.tile(np.concatenate([-s, s], axis=1), (1, 2))
    scale = HEAD_DIM ** -0.5
    table = np.concatenate([cos_t * scale, sin_t * scale, cos_t, sin_t], axis=1)
    return jnp.asarray(table, F32)


def _pad_slots(parts, axis):
    out = []
    for a in parts:
        pad = [(0, 0)] * a.ndim
        pad[axis] = (0, LORA_SLOT - a.shape[axis])
        out.append(jnp.pad(a, pad))
    return jnp.concatenate(out, axis=axis)


def kernel(x, p, mix_pre_norm, w_in, rwkv_mu, rwkv_w0, rwkv_w2, rwkv_a0, rwkv_a2, rwkv_g2,
           rwkv_k_k, rwkv_k_a, rwkv_r_k, rwkv_ln_w, rwkv_ln_b, w_attn_up, w_rwkv_up, w_out,
           mix_post_norm, ffn_pre_norm, w_ff_in, w_ff_out, ffn_post_norm, w_ple, w_ple_gate,
           ple_post_norm):
    B, S, D = x.shape
    depth = w_in.shape[0]
    nqkv = 3 * N_GROUPS * ATTN_W
    nr = 3 * RWKV_W
    lo = nqkv + nr
    h = x
    rope = _rope_tables(S)
    for i in range(depth):
        w = w_in[i]
        wl = _pad_slots([w[:, lo:lo + DECAY_LORA],
                         w[:, lo + DECAY_LORA:lo + DECAY_LORA + ICLR_LORA],
                         w[:, lo + DECAY_LORA + ICLR_LORA:lo + DECAY_LORA + ICLR_LORA + GATE_LORA]], 1)
        ng = lo + DECAY_LORA + ICLR_LORA + GATE_LORA
        mu = rwkv_mu[i]
        mul = _pad_slots([mu[nr:nr + DECAY_LORA], mu[nr + DECAY_LORA:nr + DECAY_LORA + ICLR_LORA],
                          mu[nr + DECAY_LORA + ICLR_LORA:]], 0)
        q0, q1, q2, rkv, lora, gates = _in_proj(
            h, mix_pre_norm[i][None, :], w[:, :nqkv].astype(BF16),
            w[:, nqkv:lo].astype(BF16), wl.astype(BF16), w[:, ng:].astype(BF16), rope,
            mu[None, :nr], mul[None, :], tm=min(512, S))

        os_, ls_ = [], []
        for qkv_g, (_, dil) in zip((q0, q1, q2), ATTN_GROUPS):
            o, lse = _attn_group(qkv_g, dil, tok=min(4096, S))
            os_.append(o)
            ls_.append(lse)

        lora_w = _pad_slots([rwkv_w2[i], rwkv_a2[i], rwkv_g2[i]], 0).astype(BF16)
        vecs = jnp.stack([rwkv_w0[i], rwkv_a0[i], rwkv_k_k[i], rwkv_k_a[i],
                          rwkv_r_k[i].reshape(-1), rwkv_ln_w[i], rwkv_ln_b[i],
                          jnp.zeros((RWKV_W,), F32)])
        orw = _rwkv(rkv, lora, lora_w, vecs, tt=min(1024, S))

        tm = min(512, S)
        weights = (w_attn_up[i].astype(BF16), w_rwkv_up[i].astype(BF16), w_out[i].astype(BF16),
                   mix_post_norm[i][None, :], ffn_pre_norm[i][None, :], w_ff_in[i].astype(BF16),
                   w_ff_out[i].astype(BF16), ffn_post_norm[i][None, :], w_ple[i].astype(BF16),
                   w_ple_gate[i].astype(BF16), ple_post_norm[i][None, :])
        h2 = _tail(h.reshape(B * S, D), os_, ls_, orw.reshape(B * S, RWKV_W),
                   gates.reshape(B * S, 2 * D), p.reshape(depth, B * S, -1), i, weights, tm)
        h = h2.reshape(B, S, D)
    return h
```

```python
import functools

import numpy as np
import jax
import jax.numpy as jnp
from jax import lax
from jax.experimental import pallas as pl
from jax.experimental.pallas import tpu as pltpu

F32 = jnp.float32
BF16 = jnp.bfloat16

HEAD_DIM = 64
ATTN_GROUPS = ((128, 1), (512, 4), (2048, 16))
N_GROUPS = 3
ATTN_W = 256
ATTN_KEYS = 128
ROPE_THETA = 10000.0
RWKV_W = 512
DECAY_LORA, ICLR_LORA, GATE_LORA = 32, 32, 96
LORA_SLOT = 128
RWKV_GN_EPS = 64e-5
NORM_EPS = 1e-6
CHUNK = 64
RWKV_SUB = 256
NEG = -1e30

VMEM_LIMIT = 56 * 1024 * 1024
MXU_DIM = 256


def _cparams(sem):
    return pltpu.CompilerParams(dimension_semantics=sem, vmem_limit_bytes=VMEM_LIMIT)


def _const_spec(shape):
    nd = len(shape)
    return pl.BlockSpec(shape, lambda *_: (0,) * nd, pipeline_mode=pl.Buffered(1))


def _dot(a, b):
    return jnp.dot(a, b, preferred_element_type=F32)


def _dot_nt(a, b):
    return lax.dot_general(a, b, (((1,), (1,)), ((), ())), preferred_element_type=F32)


def _dot_tn(a, b):
    return lax.dot_general(a, b, (((0,), (0,)), ((), ())), preferred_element_type=F32)


def _split3(x):
    hi = x.astype(BF16)
    r1 = x - hi.astype(F32)
    mid = r1.astype(BF16)
    lo = (r1 - mid.astype(F32)).astype(BF16)
    return hi, mid, lo


def _dot_exact_rhs(x, rhs_bf16, passes):
    parts = _split3(x)[:passes]
    acc = _dot(parts[0], rhs_bf16)
    for p in parts[1:]:
        acc = acc + _dot(p, rhs_bf16)
    return acc


def _exact_lhs_dot(lhs_bf16, x, passes):
    parts = _split3(x)[:passes]
    acc = _dot(lhs_bf16, parts[0])
    for p in parts[1:]:
        acc = acc + _dot(lhs_bf16, p)
    return acc


def _rms(x, g):
    ms = jnp.mean(x * x, axis=-1, keepdims=True)
    return x * lax.rsqrt(ms + NORM_EPS) * g


def _in_proj_kernel(x_ref, g_ref, wqkv_ref, wr_ref, wl_ref, wg_ref, rope_ref,
                    mur_ref, mul_ref, qkv0_ref, qkv1_ref, qkv2_ref, rkv_ref, lora_ref, gate_ref,
                    carry_r, carry_l, qkv_scr):
    s = pl.program_id(1)
    tm = x_ref.shape[0]

    @pl.when(s == 0)
    def _():
        carry_r[...] = jnp.zeros_like(carry_r)
        carry_l[...] = jnp.zeros_like(carry_l)

    u = _rms(x_ref[...], g_ref[...]).astype(BF16)
    gate_ref[...] = jax.nn.sigmoid(_dot(u, wg_ref[...])).astype(BF16)

    row0 = lax.broadcasted_iota(jnp.int32, (tm, 1), 0) == 0

    def shift_lerp(z_, carry, mu):
        zs = jnp.where(row0, carry[0:1, :], pltpu.roll(z_, 1, 0))
        carry[0:1, :] = z_[tm - 1:tm, :]
        return z_ + (zs - z_) * mu

    rkv_ref[...] = shift_lerp(_dot(u, wr_ref[...]), carry_r, mur_ref[...])

    z = _dot(u, wqkv_ref[...])
    lane = lax.broadcasted_iota(jnp.int32, (1, 128), 1)
    first_half = (lane % HEAD_DIM) < (HEAD_DIM // 2)
    gw = 3 * ATTN_W
    per_group = gw // 128
    halves = ATTN_W // 128
    for c in range(z.shape[1] // 128):
        typ, rest = divmod(c, N_GROUPS * halves)
        g, half = divmod(rest, halves)
        dst = typ * halves + half
        xc = z[:, c * 128:(c + 1) * 128]
        if typ < 2:
            t0 = 0 if typ == 0 else 256
            rot = jnp.where(first_half, pltpu.roll(xc, 96, 1), pltpu.roll(xc, 32, 1))
            xc = xc * rope_ref[:, t0:t0 + 128] + rot * rope_ref[:, t0 + 128:t0 + 256]
        if g == 0:
            qkv0_ref[0, :, dst * 128:(dst + 1) * 128] = xc.astype(BF16)
        else:
            qkv_scr[(g - 1) * per_group + dst] = xc
    for g, out_ref in ((1, qkv1_ref), (2, qkv2_ref)):
        dil = out_ref.shape[0]
        for r in range(dil):
            for cb in range(per_group):
                out_ref[r, :, cb * 128:(cb + 1) * 128] = qkv_scr[
                    (g - 1) * per_group + cb, pl.ds(r, tm // dil, stride=dil), :].astype(BF16)

    lora_ref[...] = shift_lerp(_dot(u, wl_ref[...]), carry_l, mul_ref[...])


def _in_proj(x, g, wqkv, wr, wl, wg, rope, mur, mul, tm):
    B, S, D = x.shape
    gw = 3 * ATTN_W
    row = lambda w: pl.BlockSpec((None, tm, w), lambda b, s: (b, s, 0))
    dils = [d for _, d in ATTN_GROUPS]
    qkv_specs = [pl.BlockSpec((None, d, tm // d, gw), lambda b, s: (b, 0, s, 0)) for d in dils]
    qkv_shapes = [jax.ShapeDtypeStruct((B, d, S // d, gw), BF16) for d in dils]
    return pl.pallas_call(
        _in_proj_kernel,
        grid=(B, S // tm),
        in_specs=[row(D), _const_spec(g.shape), _const_spec(wqkv.shape),
                  _const_spec(wr.shape), _const_spec(wl.shape), _const_spec(wg.shape),
                  pl.BlockSpec((tm, rope.shape[1]), lambda b, s: (s, 0)),
                  _const_spec(mur.shape), _const_spec(mul.shape)],
        out_specs=qkv_specs + [row(wr.shape[1]), row(wl.shape[1]), row(wg.shape[1])],
        out_shape=qkv_shapes + [jax.ShapeDtypeStruct((B, S, wr.shape[1]), F32),
                                jax.ShapeDtypeStruct((B, S, wl.shape[1]), F32),
                                jax.ShapeDtypeStruct((B, S, wg.shape[1]), BF16)],
        scratch_shapes=[pltpu.VMEM((8, wr.shape[1]), F32), pltpu.VMEM((8, wl.shape[1]), F32),
                        pltpu.VMEM(((N_GROUPS - 1) * gw // 128, tm, 128), F32)],
        compiler_params=_cparams(("arbitrary", "arbitrary")),
        name="in_proj",
    )(x, g, wqkv, wr, wl, wg, rope, mur, mul)


def _attn_kernel(cur_ref, kp_ref, vp_ref, o_ref, lse_ref, *, dil, nsub):
    i = pl.program_id(1)
    kw = 2 * ATTN_KEYS
    col = lax.broadcasted_iota(jnp.int32, (ATTN_KEYS, kw), 1)
    rowi = lax.broadcasted_iota(jnp.int32, (ATTN_KEYS, kw), 0)
    diff = col - rowi
    band = (diff >= 0) & (diff <= ATTN_KEYS)
    lane = lax.broadcasted_iota(jnp.int32, (1, ATTN_W), 1)
    hmask = [(lane // HEAD_DIM) == h for h in range(ATTN_W // HEAD_DIM)]
    zero_q = jnp.zeros((ATTN_KEYS, ATTN_W), BF16)
    low_half = lax.broadcasted_iota(jnp.int32, (1, 128), 1) < HEAD_DIM
    qs, ks, vs = slice(0, ATTN_W), slice(ATTN_W, 2 * ATTN_W), slice(2 * ATTN_W, 3 * ATTN_W)

    def load(u):
        r, sb = (u, 0) if nsub == 1 else ((0, u) if dil == 1 else (u // nsub, u % nsub))
        rows = pl.ds(pl.multiple_of(sb * ATTN_KEYS, ATTN_KEYS), ATTN_KEYS)
        q = cur_ref[r, rows, qs]
        if nsub == 1:
            k_lo, v_lo = kp_ref[r], vp_ref[r]
        else:
            lo = pl.ds(pl.multiple_of(jnp.maximum(sb - 1, 0) * ATTN_KEYS, ATTN_KEYS), ATTN_KEYS)
            first = sb == 0
            k_lo = jnp.where(first, kp_ref[r], cur_ref[r, lo, ks])
            v_lo = jnp.where(first, vp_ref[r], cur_ref[r, lo, vs])
        k = jnp.concatenate([k_lo, cur_ref[r, rows, ks]], axis=0)
        v = jnp.concatenate([v_lo, cur_ref[r, rows, vs]], axis=0)
        valid = band & ((col >= ATTN_KEYS) | (i > 0) | (sb > 0))
        return r, sb, q, k, v, valid

    def body(j, carry):
        us = [load(2 * j), load(2 * j + 1)]
        scs = [[jnp.where(valid, _dot_nt(jnp.where(hm, q, zero_q), k), NEG) for hm in hmask]
               for (_, _, q, k, _, valid) in us]
        ms = [[jnp.max(sc, axis=-1, keepdims=True) for sc in row] for row in scs]
        es = [[jnp.exp(sc - m) for sc, m in zip(r1, r2)] for r1, r2 in zip(scs, ms)]
        dens = [[jnp.sum(e, axis=-1, keepdims=True) for e in row] for row in es]
        ohs = [[_dot(e.astype(BF16), u_[4]) for e in row] for row, u_ in zip(es, us)]
        for (r, sb, *_), oh_r, den_r, m_r in zip(us, ohs, dens, ms):
            if dil == 1:
                dst = pl.ds(pl.multiple_of(sb * ATTN_KEYS, ATTN_KEYS), ATTN_KEYS)
            else:
                dst = pl.ds(r + sb * (ATTN_KEYS * dil), ATTN_KEYS, stride=dil)
            for cb in range(ATTN_W // 128):
                cols = slice(cb * 128, (cb + 1) * 128)
                h0, h1 = 2 * cb, 2 * cb + 1
                o_ref[cb, dst, :] = jnp.where(low_half, oh_r[h0][:, cols] / den_r[h0],
                                              oh_r[h1][:, cols] / den_r[h1])
                lse_ref[cb, dst, :] = jnp.where(low_half, m_r[h0] + jnp.log(den_r[h0]),
                                                m_r[h1] + jnp.log(den_r[h1]))
        return carry

    lax.fori_loop(0, dil * nsub // 2, body, 0, unroll=2)


def _attn_group(qkv_g, dil, tok):
    B, _, n, gw = qkv_g.shape
    S = n * dil
    tq = tok // dil
    nsub = tq // ATTN_KEYS
    cur = pl.BlockSpec((None, dil, tq, gw), lambda b, i: (b, 0, i, 0))
    prev = lambda c: pl.BlockSpec((None, dil, ATTN_KEYS, ATTN_W),
                                  lambda b, i: (b, 0, jnp.maximum(i * nsub - 1, 0), c))
    out = pl.BlockSpec((None, ATTN_W // 128, tok, 128), lambda b, i: (b, 0, i, 0))
    return pl.pallas_call(
        functools.partial(_attn_kernel, dil=dil, nsub=nsub),
        grid=(B, S // tok),
        in_specs=[cur, prev(1), prev(2)],
        out_specs=[out, out],
        out_shape=[jax.ShapeDtypeStruct((B, ATTN_W // 128, S, 128), F32)] * 2,
        compiler_params=_cparams(("arbitrary", "arbitrary")),
        name=f"attn_d{dil}",
    )(qkv_g, qkv_g, qkv_g)


def _bd(x):
    lane = lax.broadcasted_iota(jnp.int32, (1, x.shape[1]), 1)
    m0 = (lane % 128) < HEAD_DIM
    z = jnp.zeros_like(x)
    return jnp.concatenate([jnp.where(m0, x, z), jnp.where(m0, z, x)], axis=0)


def _rwkv_kernel(rkv_ref, lora_ref, lw_ref, vec_ref, ones_ref, tri_ref,
                 o_ref, st_ref, y_scr):
    s = pl.program_id(1)
    nsub = rkv_ref.shape[0] // RWKV_SUB

    @pl.when(s == 0)
    def _():
        st_ref[...] = jnp.zeros_like(st_ref)

    st_box = [[st_ref[p] for p in range(RWKV_W // 128)]]
    subs = [_rwkv_sub(i * RWKV_SUB, rkv_ref, lora_ref, lw_ref, vec_ref, ones_ref, tri_ref, o_ref,
                      y_scr, st_box) for i in range(nsub)]

    def advance_to(gen, tag):
        while next(gen) != tag:
            pass

    advance_to(subs[0], "prologue done")
    for i in range(nsub):
        main_done, prev_done, next_done = False, i == 0, i + 1 == nsub
        while not (main_done and prev_done and next_done):
            if not main_done:
                main_done = next(subs[i]) == "main done"
            if not prev_done:
                prev_done = next(subs[i - 1], None) is None
            if not next_done:
                next_done = next(subs[i + 1]) == "prologue done"
    for _ in subs[nsub - 1]:
        pass
    for p, st in enumerate(st_box[0]):
        st_ref[p] = st


def _rwkv_sub(base, rkv_ref, lora_ref, lw_ref, vec_ref, ones_ref, tri_ref, o_ref, y_scr, st_box):
    tt = RWKV_SUB
    nchunk = tt // CHUNK
    npair = RWKV_W // 128
    span = slice(base, base + tt)

    w0, a0, k_k, k_a, r_k, ln_w, ln_b = [vec_ref[j:j + 1, :] for j in range(7)]
    r = rkv_ref[span, 0:RWKV_W]
    k = rkv_ref[span, RWKV_W:2 * RWKV_W]
    v = rkv_ref[span, 2 * RWKV_W:3 * RWKV_W]
    xw = lora_ref[span, 0:LORA_SLOT]
    xa = lora_ref[span, LORA_SLOT:2 * LORA_SLOT]
    xg = lora_ref[span, 2 * LORA_SLOT:3 * LORA_SLOT]
    ones = ones_ref[...]
    ow = ones.shape[0]

    def head_sum(x, passes):
        return jnp.concatenate([_dot_exact_rhs(x[:, c:c + ow], ones, passes)
                                for c in range(0, RWKV_W, ow)], axis=1)

    def lora_dot(x, j):
        return _dot(x.astype(BF16), lw_ref[j * LORA_SLOT:(j + 1) * LORA_SLOT, :])

    y_w = -(w0 + lora_dot(jnp.tanh(xw), 0))
    softplus = jnp.maximum(y_w, 0.0) + jnp.log(1.0 + jnp.exp(-jnp.abs(y_w)))
    ld = -jnp.exp(-softplus - 0.5)
    yield "prologue"
    cum = _exact_lhs_dot(tri_ref[...], ld, 2)
    yield "prologue"
    a = jax.nn.sigmoid(a0 + lora_dot(xa, 1))
    g = lora_dot(jax.nn.sigmoid(xg), 2)
    yield "prologue"
    kk = k * k_k
    kk = kk / jnp.maximum(jnp.sqrt(head_sum(kk * kk, 1)), 1e-12)
    yield "prologue"
    kp = k * (1.0 + (a - 1.0) * k_a)
    avec = -kk
    bvec = kk * a
    yield "prologue"

    lane = lax.broadcasted_iota(jnp.int32, (1, 128), 1)
    m0 = lane < HEAD_DIM
    ri = lax.broadcasted_iota(jnp.int32, (CHUNK, 128), 0)
    ci = lax.broadcasted_iota(jnp.int32, (CHUNK, 128), 1) % HEAD_DIM
    strict = ci < ri
    incl = ci <= ri
    eye = ci == ri
    blk16 = (ci // 16) == (ri // 16)
    blk32 = (ci // 32) == (ri // 32)

    units = [(slice(c * CHUNK, (c + 1) * CHUNK), slice(p * 128, (p + 1) * 128))
             for c in range(nchunk) for p in range(npair)]
    each = lambda fn, *lists: [fn(*args) for args in zip(*lists)]
    cat0 = lambda *xs: jnp.concatenate(xs, axis=0)
    cat1 = lambda *xs: jnp.concatenate(xs, axis=1)

    def prep(u):
        rows, cols = u
        cm = cum[rows, cols]
        c_last = cm[CHUNK - 1:CHUNK, :]
        w_in = jnp.exp(cm)
        w_ex = jnp.exp(cm - ld[rows, cols])
        w_inv = jnp.exp(-cm)
        w_out = jnp.exp(c_last - cm)
        r_f = r[rows, cols] * w_in
        return dict(
            r_f=r_f, r_t=r_f.astype(BF16),
            a_t=(avec[rows, cols] * w_ex).astype(BF16),
            b_t=(bvec[rows, cols] * w_inv).astype(BF16),
            k_t=(kp[rows, cols] * w_inv).astype(BF16),
            b_o=(bvec[rows, cols] * w_out).astype(BF16),
            k_o=(kp[rows, cols] * w_out).astype(BF16),
            vv=v[rows, cols].astype(BF16),
            wc=jnp.where(eye, jnp.exp(c_last), 0.0))

    P = []
    for c in range(nchunk):
        P += each(prep, units[c * npair:(c + 1) * npair])
        yield "prologue"
    yield "prologue done"

    sc = each(lambda d: _dot_nt(cat0(d["a_t"], d["r_t"]), cat0(_bd(d["b_t"]), _bd(d["k_t"]))), P)
    yield "main"
    l_ab = [jnp.where(strict, x[0:CHUNK, 0:128], 0.0) for x in sc]
    m_ak = [jnp.where(strict, x[0:CHUNK, 128:256], 0.0) for x in sc]
    m_rb = [jnp.where(incl, x[CHUNK:, 0:128], 0.0).astype(BF16) for x in sc]
    m_rk = [jnp.where(incl, x[CHUNK:, 128:256], 0.0) for x in sc]

    mv = each(lambda ak, rk, d: _dot(cat0(ak, rk).astype(BF16), _bd(d["vv"])), m_ak, m_rk, P)
    yield "main"
    makv = [x[0:CHUNK].astype(BF16) for x in mv]
    mrkv = [x[CHUNK:] for x in mv]

    pw = [jnp.where(blk16, x, 0.0).astype(BF16) for x in l_ab]
    tmat = [jnp.where(eye, 1.0, 0.0) for _ in units]
    for it in range(4):
        if it < 3:
            z = each(lambda a, t: _dot(a, _bd(cat1(a, t.astype(BF16)))), pw, tmat)
            pw = [x[:, 0:128].astype(BF16) for x in z]
            tmat = each(lambda t, x: t + x[:, 128:256], tmat, z)
        else:
            tmat = each(lambda a, t: t + _dot(a, _bd(t.astype(BF16))), pw, tmat)
        yield "main"
    for sel in (lambda x: jnp.where(blk32 & ~blk16, x, 0.0), lambda x: jnp.where(blk32, 0.0, x)):
        off = [sel(x).astype(BF16) for x in l_ab]
        t16 = [t.astype(BF16) for t in tmat]
        ot = each(lambda o, t: _dot(o, _bd(t)).astype(BF16), off, t16)
        yield "main"
        tmat = each(lambda t, t6, o: t + _dot(t6, _bd(o)), tmat, t16, ot)
        yield "main"

    xh = each(lambda t, d, m: _dot(t.astype(BF16), _bd(cat1(d["a_t"], m))), tmat, P, makv)
    yield "main"
    ahv = [x.astype(BF16) for x in xh]

    mr = each(lambda m, x: _dot(m, _bd(x)), m_rb, ahv)
    yield "main"
    r_hat = each(lambda d, x: d["r_f"] + x[:, 0:128], P, mr)
    y_intra = each(lambda x, y: x[:, 128:256] + y, mr, mrkv)

    gh = each(lambda d, x: _dot_tn(cat0(d["b_o"], d["k_o"]),
                                   cat0(x, cat1(jnp.zeros_like(d["vv"]), d["vv"]))), P, ahv)
    gt = each(lambda x, d: jnp.where(m0, x[0:CHUNK, 0:128], x[CHUNK:, 0:128]) + d["wc"], gh, P)
    ht = [jnp.where(m0, x[0:CHUNK, 128:256], x[CHUNK:, 128:256]) for x in gh]
    lhs = each(lambda a, b: cat0(a, b).astype(BF16), r_hat, gt)
    yield "main done"

    st = st_box[0]
    for c in range(nchunk):
        sl = slice(c * npair, (c + 1) * npair)
        ys = each(lambda a, s_: _dot(a, _bd(s_.astype(BF16))), lhs[sl], st)
        for (rows, cols), y_, yi in zip(units[sl], ys, y_intra[sl]):
            y_scr[base + rows.start:base + rows.stop, cols] = y_[0:CHUNK] + yi
        st = each(lambda y_, h_: y_[CHUNK:] + h_, ys, ht[sl])
        yield "state pass"
    st_box[0] = st

    y = y_scr[span, :]
    inv_n = 1.0 / HEAD_DIM
    mean = head_sum(y, 2) * inv_n
    yield "epilogue"
    yc = y - mean
    var = head_sum(yc * yc, 1) * inv_n
    yield "epilogue"
    yn = yc * lax.rsqrt(var + RWKV_GN_EPS) * ln_w + ln_b
    bonus = head_sum(r * kp * r_k, 1) * v
    o_ref[span, :] = ((yn + bonus) * g).astype(o_ref.dtype)


def _rwkv(rkv, lora, lora_w, vecs, tt):
    B, S, _ = rkv.shape
    ones = jnp.asarray(np.kron(np.eye(MXU_DIM // HEAD_DIM), np.ones((HEAD_DIM, HEAD_DIM))), BF16)
    tri = jnp.asarray(np.kron(np.eye(RWKV_SUB // CHUNK), np.tril(np.ones((CHUNK, CHUNK)))), BF16)
    row = lambda w: pl.BlockSpec((None, tt, w), lambda b, s: (b, s, 0))
    return pl.pallas_call(
        _rwkv_kernel,
        grid=(B, S // tt),
        in_specs=[row(rkv.shape[2]), row(lora.shape[2]), _const_spec(lora_w.shape),
                  _const_spec(vecs.shape), _const_spec(ones.shape), _const_spec(tri.shape)],
        out_specs=row(RWKV_W),
        out_shape=jax.ShapeDtypeStruct((B, S, RWKV_W), BF16),
        scratch_shapes=[pltpu.VMEM((RWKV_W // 128, HEAD_DIM, 128), F32),
                        pltpu.VMEM((tt, RWKV_W), F32)],
        compiler_params=_cparams(("arbitrary", "arbitrary")),
        name="rwkv7",
    )(rkv, lora, lora_w, vecs, ones, tri)


def _tail_kernel(x_ref, o0, o1, o2, l0, l1, l2, orw_ref, gate_ref, p_ref,
                 wau_ref, wru_ref, wout_ref, gpost_ref, gpre_ref, wfi_ref, wfo_ref, gffn_ref,
                 wple_ref, wpg_ref, gple_ref, out_ref):
    full = lambda ref: jnp.concatenate([ref[c] for c in range(ref.shape[0])], axis=1)
    ls = [full(l0), full(l1), full(l2)]
    m = jnp.maximum(jnp.maximum(ls[0], ls[1]), ls[2])
    ws = [jnp.exp(l - m) for l in ls]
    o_attn = (ws[0] * full(o0) + ws[1] * full(o1) + ws[2] * full(o2)) / (ws[0] + ws[1] + ws[2])
    d = x_ref.shape[1]
    ga = gate_ref[:, 0:d].astype(F32)
    gr = gate_ref[:, d:2 * d].astype(F32)
    merged = ga * _dot(o_attn.astype(BF16), wau_ref[...]) + gr * _dot(orw_ref[...], wru_ref[...])
    mo = _dot(merged.astype(BF16), wout_ref[...])
    h = x_ref[...] + _rms(mo, gpost_ref[...])

    f = _rms(h, gpre_ref[...]).astype(BF16)
    dff = wfi_ref.shape[1]
    nck = 4
    ck = dff // nck
    acc = jnp.zeros(h.shape, F32)
    for j in range(nck):
        a = jnp.maximum(_dot(f, wfi_ref[:, j * ck:(j + 1) * ck]), 0.0)
        acc = acc + _dot((a * a).astype(BF16), wfo_ref[j * ck:(j + 1) * ck, :])
    h = h + _rms(acc, gffn_ref[...])
    e = _dot(p_ref[...].astype(BF16), wple_ref[...]) * jax.nn.sigmoid(
        _dot(h.astype(BF16), wpg_ref[...]))
    out_ref[...] = h + _rms(e, gple_ref[...])


def _tail(x2, os_, ls_, orw, gates, p3, layer, weights, tm):
    T, D = x2.shape
    row = lambda w: pl.BlockSpec((tm, w), lambda i: (i, 0))
    per_b = os_[0].shape[2] // tm
    attn = pl.BlockSpec((None, ATTN_W // 128, tm, 128), lambda i: (i // per_b, 0, i % per_b, 0))
    p_spec = pl.BlockSpec((None, tm, p3.shape[2]), lambda i: (layer, i, 0))
    return pl.pallas_call(
        _tail_kernel,
        grid=(T // tm,),
        in_specs=[row(D)] + [attn] * 6 + [row(RWKV_W), row(2 * D), p_spec]
                 + [_const_spec(w.shape) for w in weights],
        out_specs=row(D),
        out_shape=jax.ShapeDtypeStruct((T, D), F32),
        compiler_params=_cparams(("arbitrary",)),
        name="tail",
    )(x2, *os_, *ls_, orw, gates, p3, *weights)


def _rope_tables(S):
    inv_freq = ROPE_THETA ** (-np.arange(0, HEAD_DIM, 2, dtype=np.float64) / HEAD_DIM)
    ang = np.arange(S, dtype=np.float64)[:, None] * inv_freq[None, :]
    c, s = np.cos(ang), np.sin(ang)
    cos_t = np.tile(np.concatenate([c, c], axis=1), (1, 2))
    sin_t = np.tile(np.concatenate([-s, s], axis=1), (1, 2))
    scale = HEAD_DIM ** -0.5
    table = np.concatenate([cos_t * scale, sin_t * scale, cos_t, sin_t], axis=1)
    return jnp.asarray(table, F32)


def _pad_slots(parts, axis):
    out = []
    for a in parts:
        pad = [(0, 0)] * a.ndim
        pad[axis] = (0, LORA_SLOT - a.shape[axis])
        out.append(jnp.pad(a, pad))
    return jnp.concatenate(out, axis=axis)


def kernel(x, p, mix_pre_norm, w_in, rwkv_mu, rwkv_w0, rwkv_w2, rwkv_a0, rwkv_a2, rwkv_g2,
           rwkv_k_k, rwkv_k_a, rwkv_r_k, rwkv_ln_w, rwkv_ln_b, w_attn_up, w_rwkv_up, w_out,
           mix_post_norm, ffn_pre_norm, w_ff_in, w_ff_out, ffn_post_norm, w_ple, w_ple_gate,
           ple_post_norm):
    B, S, D = x.shape
    depth = w_in.shape[0]
    nqkv = 3 * N_GROUPS * ATTN_W
    nr = 3 * RWKV_W
    lo = nqkv + nr
    h = x
    rope = _rope_tables(S)
    for i in range(depth):
        w = w_in[i]
        wl = _pad_slots([w[:, lo:lo + DECAY_LORA],
                         w[:, lo + DECAY_LORA:lo + DECAY_LORA + ICLR_LORA],
                         w[:, lo + DECAY_LORA + ICLR_LORA:lo + DECAY_LORA + ICLR_LORA + GATE_LORA]], 1)
        ng = lo + DECAY_LORA + ICLR_LORA + GATE_LORA
        mu = rwkv_mu[i]
        mul = _pad_slots([mu[nr:nr + DECAY_LORA], mu[nr + DECAY_LORA:nr + DECAY_LORA + ICLR_LORA],
                          mu[nr + DECAY_LORA + ICLR_LORA:]], 0)
        q0, q1, q2, rkv, lora, gates = _in_proj(
            h, mix_pre_norm[i][None, :], w[:, :nqkv].astype(BF16),
            w[:, nqkv:lo].astype(BF16), wl.astype(BF16), w[:, ng:].astype(BF16), rope,
            mu[None, :nr], mul[None, :], tm=min(512, S))

        os_, ls_ = [], []
        for qkv_g, (_, dil) in zip((q0, q1, q2), ATTN_GROUPS):
            o, lse = _attn_group(qkv_g, dil, tok=min(4096, S))
            os_.append(o)
            ls_.append(lse)

        lora_w = _pad_slots([rwkv_w2[i], rwkv_a2[i], rwkv_g2[i]], 0).astype(BF16)
        vecs = jnp.stack([rwkv_w0[i], rwkv_a0[i], rwkv_k_k[i], rwkv_k_a[i],
                          rwkv_r_k[i].reshape(-1), rwkv_ln_w[i], rwkv_ln_b[i],
                          jnp.zeros((RWKV_W,), F32)])
        orw = _rwkv(rkv, lora, lora_w, vecs, tt=min(1024, S))

        tm = min(512, S)
        weights = (w_attn_up[i].astype(BF16), w_rwkv_up[i].astype(BF16), w_out[i].astype(BF16),
                   mix_post_norm[i][None, :], ffn_pre_norm[i][None, :], w_ff_in[i].astype(BF16),
                   w_ff_out[i].astype(BF16), ffn_post_norm[i][None, :], w_ple[i].astype(BF16),
                   w_ple_gate[i].astype(BF16), ple_post_norm[i][None, :])
        h2 = _tail(h.reshape(B * S, D), os_, ls_, orw.reshape(B * S, RWKV_W),
                   gates.reshape(B * S, 2 * D), p.reshape(depth, B * S, -1), i, weights, tm)
        h = h2.reshape(B, S, D)
    return h
```
